```python
import math
import jax, jax.numpy as jnp
from jax import lax
import numpy as np

D_MODEL = 1024
BATCH = 2
SEQ = 8192
DEPTH = 2

CONV_WIDTH = D_MODEL // 2
CONV_K = 3
N_HEADS = 8
QK_NOPE_DIM = 64
QK_ROPE_DIM = 32
V_HEAD_DIM = 64
QK_HEAD_DIM = QK_NOPE_DIM + QK_ROPE_DIM
Q_LORA_RANK = 256
KV_LORA_RANK = 128
ROPE_THETA = 10000.0
Q_BLOCK = 128
N_BRANCHES = 2
IN_COLS = 3 * CONV_WIDTH + Q_LORA_RANK + KV_LORA_RANK + QK_ROPE_DIM + N_BRANCHES * D_MODEL
N_EXPERTS = 16
N_GROUPS = 4
EXPERTS_PER_GROUP = N_EXPERTS // N_GROUPS
TOP_K = 2
D_EXPERT = 256
D_SHARED = 256
EPS = 1e-6

kernel_name = "hybrid_conv_mla_grouped_moe_adaln_encoder"


def rmsnorm(x, g):
    xf = x.astype(jnp.float32)
    y = xf * lax.rsqrt(jnp.mean(xf * xf, axis=-1, keepdims=True) + EPS)
    return (y * g.astype(jnp.float32)).astype(x.dtype)


def rope(x, positions):
    half = QK_ROPE_DIM // 2
    inv_freq = ROPE_THETA ** (-jnp.arange(half, dtype=jnp.float32) / half)
    ang = positions.astype(jnp.float32)[..., None] * inv_freq
    cos = jnp.cos(ang)[:, :, None, :]
    sin = jnp.sin(ang)[:, :, None, :]
    xf = x.astype(jnp.float32)
    x1, x2 = xf[..., :half], xf[..., half:]
    out = jnp.concatenate([x1 * cos - x2 * sin, x2 * cos + x1 * sin], axis=-1)
    return out.astype(x.dtype)


def short_conv(u, w):
    pad = CONV_K // 2
    return lax.conv_general_dilated(
        u, w[:, None, :], window_strides=(1,), padding=[(pad, pad)],
        dimension_numbers=('NWC', 'WIO', 'NWC'), feature_group_count=u.shape[-1])


def block_attention(q, k, v):
    b, s, h, dq = q.shape
    nblk = s // Q_BLOCK
    scale = QK_HEAD_DIM ** -0.5
    qb = q.reshape(b, nblk, Q_BLOCK, h, dq).transpose(1, 0, 2, 3, 4)

    def one_block(qblk):
        sc = jnp.einsum('bqhd,bkhd->bhqk', qblk, k).astype(jnp.float32) * scale
        p = jax.nn.softmax(sc, axis=-1)
        return jnp.einsum('bhqk,bkhd->bqhd', p.astype(v.dtype), v)

    o = lax.map(one_block, qb)
    return o.transpose(1, 0, 2, 3, 4).reshape(b, s, h, v.shape[-1])


def mixer(h, positions, w_in, conv_w, g_q_a, w_uq, g_kv_a, w_ukv, g_qn, g_kn,
          w_conv_out, w_attn_out, w_o):
    b, s, _ = h.shape
    sizes = [CONV_WIDTH, CONV_WIDTH, CONV_WIDTH, Q_LORA_RANK, KV_LORA_RANK, QK_ROPE_DIM, D_MODEL]
    splits = [int(i) for i in np.cumsum(sizes)]
    proj = h @ w_in
    cb, cc, cx, cq, ckv, k_pe, ga, gb = jnp.split(proj, splits, axis=-1)

    ya = (cb * short_conv(cc * cx, conv_w)) @ w_conv_out

    q = (rmsnorm(cq, g_q_a) @ w_uq).reshape(b, s, N_HEADS, QK_HEAD_DIM)
    kv = (rmsnorm(ckv, g_kv_a) @ w_ukv).reshape(b, s, N_HEADS, QK_NOPE_DIM + V_HEAD_DIM)
    k_nope, v = kv[..., :QK_NOPE_DIM], kv[..., QK_NOPE_DIM:]
    k = jnp.concatenate(
        [k_nope, jnp.broadcast_to(k_pe[:, :, None, :], (b, s, N_HEADS, QK_ROPE_DIM))], axis=-1)
    q = rmsnorm(q, g_qn)
    k = rmsnorm(k, g_kn)
    q = jnp.concatenate([q[..., :QK_NOPE_DIM], rope(q[..., QK_NOPE_DIM:], positions)], axis=-1)
    k = jnp.concatenate([k[..., :QK_NOPE_DIM], rope(k[..., QK_NOPE_DIM:], positions)], axis=-1)
    o = block_attention(q, k, v).reshape(b, s, N_HEADS * V_HEAD_DIM)
    yb = o @ w_attn_out

    merged = jax.nn.sigmoid(ga) * ya + jax.nn.sigmoid(gb) * yb
    return merged @ w_o


def moe(h, w_router, b_router, w_e_gate, w_e_up, w_e_down, w_s_gate, w_s_up, w_s_down):
    scores = jax.nn.sigmoid(jnp.einsum('bsd,de->bse', h, w_router).astype(jnp.float32))
    biased = scores + b_router.astype(jnp.float32)
    grouped = biased.reshape(biased.shape[:-1] + (N_GROUPS, EXPERTS_PER_GROUP))
    group_score = lax.top_k(grouped, TOP_K)[0].sum(axis=-1)
    sel_group = jnp.argmax(group_score, axis=-1)
    expert_group = jnp.arange(N_EXPERTS) // EXPERTS_PER_GROUP
    masked = jnp.where(expert_group == sel_group[..., None], biased, -jnp.inf)
    _, idx = lax.top_k(masked, TOP_K)
    sel = jnp.take_along_axis(scores, idx, axis=-1)
    wts = sel / jnp.sum(sel, axis=-1, keepdims=True)
    combine = jnp.sum(jax.nn.one_hot(idx, N_EXPERTS, dtype=jnp.float32) * wts[..., None], axis=-2)

    a = jax.nn.silu(jnp.einsum('bsd,edf->bsef', h, w_e_gate)) * jnp.einsum('bsd,edf->bsef', h, w_e_up)
    a = a * combine.astype(h.dtype)[..., None]
    routed = jnp.einsum('bsef,efd->bsd', a, w_e_down)
    shared = (jax.nn.silu(h @ w_s_gate) * (h @ w_s_up)) @ w_s_down
    return routed + shared


def setup_inputs(seed: int = 0) -> dict:
    key = jax.random.key(seed)
    ks = jax.random.split(key, 32)
    f32 = jnp.float32

    def nrm(k, shape, fan_in, mult=1.0):
        return jax.random.normal(k, shape, f32) * (mult * fan_in ** -0.5)

    def gain(k, shape):
        return 1.0 + 0.02 * jax.random.normal(k, shape, f32)

    x = jax.random.normal(ks[0], (BATCH, SEQ, D_MODEL), f32)
    c = jax.random.normal(ks[1], (BATCH, D_MODEL), f32)
    offset = jax.random.randint(ks[2], (BATCH, 1), 0, SEQ, dtype=jnp.int32)
    positions = jnp.broadcast_to(jnp.arange(SEQ, dtype=jnp.int32)[None, :], (BATCH, SEQ)) + offset
    L = DEPTH
    return {
        'x': x,
        'c': c,
        'positions': positions,
        'w_ada': nrm(ks[3], (L, D_MODEL, 6 * D_MODEL), D_MODEL, 0.5),
        'b_ada': 0.01 * jax.random.normal(ks[4], (L, 6 * D_MODEL), f32),
        'g_mix': gain(ks[5], (L, D_MODEL)),
        'g_ffn': gain(ks[6], (L, D_MODEL)),
        'w_in': nrm(ks[7], (L, D_MODEL, IN_COLS), D_MODEL),
        'conv_w': nrm(ks[8], (L, CONV_K, CONV_WIDTH), CONV_K),
        'g_q_a': gain(ks[9], (L, Q_LORA_RANK)),
        'w_uq': nrm(ks[10], (L, Q_LORA_RANK, N_HEADS * QK_HEAD_DIM), Q_LORA_RANK),
        'g_kv_a': gain(ks[11], (L, KV_LORA_RANK)),
        'w_ukv': nrm(ks[12], (L, KV_LORA_RANK, N_HEADS * (QK_NOPE_DIM + V_HEAD_DIM)), KV_LORA_RANK),
        'g_qn': gain(ks[13], (L, QK_HEAD_DIM)),
        'g_kn': gain(ks[14], (L, QK_HEAD_DIM)),
        'w_conv_out': nrm(ks[15], (L, CONV_WIDTH, D_MODEL), CONV_WIDTH),
        'w_attn_out': nrm(ks[16], (L, N_HEADS * V_HEAD_DIM, D_MODEL), N_HEADS * V_HEAD_DIM),
        'w_o': nrm(ks[17], (L, D_MODEL, D_MODEL), D_MODEL),
        'w_router': nrm(ks[18], (D_MODEL, N_EXPERTS), D_MODEL),
        'b_router': 0.01 * jax.random.normal(ks[19], (N_EXPERTS,), f32),
        'w_e_gate': nrm(ks[20], (L, N_EXPERTS, D_MODEL, D_EXPERT), D_MODEL),
        'w_e_up': nrm(ks[21], (L, N_EXPERTS, D_MODEL, D_EXPERT), D_MODEL),
        'w_e_down': nrm(ks[22], (L, N_EXPERTS, D_EXPERT, D_MODEL), D_EXPERT),
        'w_s_gate': nrm(ks[23], (L, D_MODEL, D_SHARED), D_MODEL),
        'w_s_up': nrm(ks[24], (L, D_MODEL, D_SHARED), D_MODEL),
        'w_s_down': nrm(ks[25], (L, D_SHARED, D_MODEL), D_SHARED),
    }


def reference(x, c, positions, w_ada, b_ada, g_mix, g_ffn, w_in, conv_w, g_q_a, w_uq,
              g_kv_a, w_ukv, g_qn, g_kn, w_conv_out, w_attn_out, w_o, w_router, b_router,
              w_e_gate, w_e_up, w_e_down, w_s_gate, w_s_up, w_s_down):
    for l in range(DEPTH):
        mod = jax.nn.silu(c) @ w_ada[l] + b_ada[l]
        sh1, sc1, gt1, sh2, sc2, gt2 = jnp.split(mod[:, None, :], 6, axis=-1)
        h = rmsnorm(x, g_mix[l]) * (1 + sc1) + sh1
        x = x + gt1 * mixer(h, positions, w_in[l], conv_w[l], g_q_a[l], w_uq[l], g_kv_a[l],
                            w_ukv[l], g_qn[l], g_kn[l], w_conv_out[l], w_attn_out[l], w_o[l])
        h = rmsnorm(x, g_ffn[l]) * (1 + sc2) + sh2
        x = x + gt2 * moe(h, w_router, b_router, w_e_gate[l], w_e_up[l], w_e_down[l],
                          w_s_gate[l], w_s_up[l], w_s_down[l])
    return x
```

```python
import functools

import jax
import jax.numpy as jnp
from jax import lax
from jax.experimental import pallas as pl
from jax.experimental.pallas import tpu as pltpu

F32 = jnp.float32
BF16 = jnp.bfloat16

D_MODEL = 1024
CONV_WIDTH = 512
N_HEADS = 8
QK_NOPE = 64
QK_ROPE = 32
V_DIM = 64
QK_DIM = QK_NOPE + QK_ROPE
Q_RANK = 256
KV_RANK = 128
ROPE_THETA = 10000.0
N_EXPERTS = 16
N_GROUPS = 4
GROUP_SIZE = N_EXPERTS // N_GROUPS
D_EXPERT = 256
D_SHARED = 256
EPS = 1e-6

HEAD_PAD = 128
V_ROWS = 80
IN_PAD = 4096
ROPE_HALF = QK_ROPE // 2

TOKEN_TILE = 512
Q_TILE = 512
K_CHUNK = 512
VMEM_LIMIT = 56 * 1024 * 1024
NEG_BIG = -1e30


def _const_spec(shape):
    nd = len(shape)
    return pl.BlockSpec(shape, lambda *_: (0,) * nd, pipeline_mode=pl.Buffered(1))


def _silu(v):
    return v * jax.nn.sigmoid(v)


def _mod_kernel(c_ref, w_ref, b_ref, o_ref):
    c = c_ref[...]
    o_ref[...] = jnp.dot(_silu(c), w_ref[...], precision=lax.Precision.HIGHEST,
                         preferred_element_type=F32) + b_ref[...]


def _modulation(c_pad, w_ada, b_ada):
    depth = w_ada.shape[0]
    nblk = w_ada.shape[2] // D_MODEL
    rows = c_pad.shape[0]
    return pl.pallas_call(
        _mod_kernel,
        out_shape=jax.ShapeDtypeStruct((depth, rows, w_ada.shape[2]), F32),
        grid=(depth, nblk),
        in_specs=[
            pl.BlockSpec((rows, D_MODEL), lambda l, j: (0, 0)),
            pl.BlockSpec((None, D_MODEL, D_MODEL), lambda l, j: (l, 0, j)),
            pl.BlockSpec((None, 1, D_MODEL), lambda l, j: (l, 0, j)),
        ],
        out_specs=pl.BlockSpec((None, rows, D_MODEL), lambda l, j: (l, 0, j)),
        compiler_params=pltpu.CompilerParams(
            dimension_semantics=("arbitrary", "arbitrary"), vmem_limit_bytes=VMEM_LIMIT),
        name="adaln_mod",
    )(c_pad, w_ada, b_ada.reshape(depth, 1, -1))


def _proj_kernel(x_ref, g_ref, sc_ref, sh_ref, win_ref, gqa_ref, wuqt_ref, gkva_ref, wuk_ref,
                 wuvt_ref, gq_ref, gk_ref, cost_ref, sint_ref, ck_ref, sa_ref, sb_ref,
                 cb_ref, u_ref, sga_ref, sgb_ref, qt_ref, k_ref, vt_ref):
    x = x_ref[...]
    r = lax.rsqrt(jnp.mean(x * x, axis=-1, keepdims=True) + EPS)
    h = (x * r) * (g_ref[...] * (1.0 + sc_ref[...])) + sh_ref[...]
    hb = h.astype(BF16)
    tm = x.shape[0]

    pc = jnp.dot(hb, win_ref[:, 0:3 * CONV_WIDTH], preferred_element_type=F32)
    cb_ref[...] = pc[:, 0:CONV_WIDTH].astype(BF16)
    u_ref[...] = (pc[:, CONV_WIDTH:2 * CONV_WIDTH] * pc[:, 2 * CONV_WIDTH:3 * CONV_WIDTH]).astype(BF16)

    pg = jnp.dot(hb, win_ref[:, 2048:IN_PAD], preferred_element_type=F32)
    sga_ref[...] = jax.nn.sigmoid(pg[:, 0:D_MODEL]).astype(BF16)
    sgb_ref[...] = jax.nn.sigmoid(pg[:, D_MODEL:2 * D_MODEL]).astype(BF16)

    lat = jnp.dot(hb, win_ref[:, 1536:2048], preferred_element_type=F32)
    cq = lat[:, 0:Q_RANK]
    ckv = lat[:, Q_RANK:Q_RANK + KV_RANK]
    kpe = lat[:, Q_RANK + KV_RANK:Q_RANK + KV_RANK + HEAD_PAD]

    cqn = (cq * lax.rsqrt(jnp.mean(cq * cq, axis=-1, keepdims=True) + EPS) * gqa_ref[...]).astype(BF16)
    ckvn = (ckv * lax.rsqrt(jnp.mean(ckv * ckv, axis=-1, keepdims=True) + EPS) * gkva_ref[...]).astype(BF16)

    nt = (((1,), (1,)), ((), ()))
    qt = lax.dot_general(wuqt_ref[...], cqn, nt, preferred_element_type=F32)
    cost = cost_ref[...]
    sint = sint_ref[...]
    gq = gq_ref[...]
    for hh in range(N_HEADS):
        base = hh * HEAD_PAD
        qh = qt[base:base + HEAD_PAD]
        rr = lax.rsqrt(jnp.sum(qh * qh, axis=0, keepdims=True) * (1.0 / QK_DIM) + EPS)
        qn = qh * rr * gq
        x1 = qn[QK_NOPE:QK_NOPE + ROPE_HALF]
        x2 = qn[QK_NOPE + ROPE_HALF:QK_DIM]
        qt_ref[base:base + QK_NOPE, :] = qn[0:QK_NOPE].astype(BF16)
        qt_ref[base + QK_NOPE:base + QK_NOPE + ROPE_HALF, :] = (x1 * cost - x2 * sint).astype(BF16)
        qt_ref[base + QK_NOPE + ROPE_HALF:base + QK_DIM, :] = (x2 * cost + x1 * sint).astype(BF16)
        qt_ref[base + QK_DIM:base + HEAD_PAD, :] = jnp.zeros((HEAD_PAD - QK_DIM, tm), BF16)

    kn = jnp.dot(ckvn, wuk_ref[...], preferred_element_type=F32)
    ck = ck_ref[...]
    sa = sa_ref[...]
    sb = sb_ref[...]
    gk = gk_ref[...]
    for hh in range(N_HEADS):
        base = hh * HEAD_PAD
        kh = kn[:, base:base + HEAD_PAD] + kpe
        rr = lax.rsqrt(jnp.sum(kh * kh, axis=-1, keepdims=True) * (1.0 / QK_DIM) + EPS)
        kk = kh * rr * gk
        ko = kk * ck + pltpu.roll(kk, HEAD_PAD - ROPE_HALF, 1) * sa + pltpu.roll(kk, ROPE_HALF, 1) * sb
        k_ref[:, base:base + HEAD_PAD] = ko.astype(BF16)

    vt = lax.dot_general(wuvt_ref[...], ckvn, nt, preferred_element_type=F32)
    for hh in range(N_HEADS):
        vt_ref[hh * V_ROWS:hh * V_ROWS + V_DIM, :] = vt[hh * V_DIM:(hh + 1) * V_DIM].astype(BF16)
        vt_ref[hh * V_ROWS + V_DIM:(hh + 1) * V_ROWS, :] = jnp.ones((V_ROWS - V_DIM, tm), BF16)


def _proj_call(x, g_mix, sc1, sh1, wp, tabs):
    b, s, _ = x.shape
    tm = TOKEN_TILE
    tile = lambda w: pl.BlockSpec((None, tm, w), lambda bi, i: (bi, i, 0))
    tile_t = lambda rws: pl.BlockSpec((None, rws, tm), lambda bi, i: (bi, 0, i))
    per_batch = pl.BlockSpec((None, 1, D_MODEL), lambda bi, i: (bi, 0, 0))
    in_specs = [
        tile(D_MODEL), _const_spec((1, D_MODEL)), per_batch, per_batch,
        _const_spec((D_MODEL, IN_PAD)), _const_spec((1, Q_RANK)),
        _const_spec((N_HEADS * HEAD_PAD, Q_RANK)), _const_spec((1, KV_RANK)),
        _const_spec((KV_RANK, N_HEADS * HEAD_PAD)), _const_spec((N_HEADS * V_DIM, KV_RANK)),
        _const_spec((HEAD_PAD, 1)), _const_spec((1, HEAD_PAD)),
        tile_t(ROPE_HALF), tile_t(ROPE_HALF), tile(HEAD_PAD), tile(HEAD_PAD), tile(HEAD_PAD),
    ]
    out_shape = [
        jax.ShapeDtypeStruct((b, s, CONV_WIDTH), BF16),
        jax.ShapeDtypeStruct((b, s, CONV_WIDTH), BF16),
        jax.ShapeDtypeStruct((b, s, D_MODEL), BF16),
        jax.ShapeDtypeStruct((b, s, D_MODEL), BF16),
        jax.ShapeDtypeStruct((b, N_HEADS * HEAD_PAD, s), BF16),
        jax.ShapeDtypeStruct((b, s, N_HEADS * HEAD_PAD), BF16),
        jax.ShapeDtypeStruct((b, N_HEADS * V_ROWS, s), BF16),
    ]
    out_specs = [tile(CONV_WIDTH), tile(CONV_WIDTH), tile(D_MODEL), tile(D_MODEL),
                 tile_t(N_HEADS * HEAD_PAD), tile(N_HEADS * HEAD_PAD), tile_t(N_HEADS * V_ROWS)]
    return pl.pallas_call(
        _proj_kernel, out_shape=out_shape, grid=(b, s // tm),
        in_specs=in_specs, out_specs=out_specs,
        compiler_params=pltpu.CompilerParams(
            dimension_semantics=("parallel", "parallel"), vmem_limit_bytes=VMEM_LIMIT),
        name="norm_in_proj",
    )(x, g_mix, sc1, sh1, wp["w_in"], wp["g_q_a"], wp["w_uq_t"], wp["g_kv_a"], wp["w_uk"],
      wp["w_uv_t"], wp["g_q"], wp["g_k"], tabs["cos_t"], tabs["sin_t"], tabs["ck"], tabs["sa"], tabs["sb"])


def _attn_kernel(qt_ref, k_ref, vt_ref, ot_ref):
    qt = qt_ref[...]
    tq = qt.shape[1]
    nchunks = k_ref.shape[0] // K_CHUNK

    def body(c, carry):
        m, acc = carry
        off = pl.multiple_of(c * K_CHUNK, K_CHUNK)
        kc = k_ref[pl.ds(off, K_CHUNK), :]
        st = jnp.dot(kc, qt, preferred_element_type=F32)
        m_new = jnp.maximum(m, jnp.max(st, axis=0, keepdims=True))
        alpha = jnp.exp(m - m_new)
        p = jnp.exp(st - m_new).astype(BF16)
        vc = vt_ref[:, pl.ds(off, K_CHUNK)]
        acc = alpha * acc + jnp.dot(vc, p, preferred_element_type=F32)
        return m_new, acc

    m0 = jnp.full((1, tq), NEG_BIG, F32)
    acc0 = jnp.zeros((V_ROWS, tq), F32)
    _, acc = lax.fori_loop(0, nchunks, body, (m0, acc0))
    ot_ref[...] = (acc[0:V_DIM] / acc[V_DIM:V_DIM + 1]).astype(BF16)


def _attn_call(qt, k, vt):
    b, s, _ = k.shape
    tq = Q_TILE
    return pl.pallas_call(
        _attn_kernel,
        out_shape=jax.ShapeDtypeStruct((b, N_HEADS * V_DIM, s), BF16),
        grid=(b, N_HEADS, s // tq),
        in_specs=[
            pl.BlockSpec((None, HEAD_PAD, tq), lambda bi, hh, i: (bi, hh, i)),
            pl.BlockSpec((None, s, HEAD_PAD), lambda bi, hh, i: (bi, 0, hh)),
            pl.BlockSpec((None, V_ROWS, s), lambda bi, hh, i: (bi, hh, 0)),
        ],
        out_specs=pl.BlockSpec((None, V_DIM, tq), lambda bi, hh, i: (bi, hh, i)),
        compiler_params=pltpu.CompilerParams(
            dimension_semantics=("parallel", "parallel", "parallel"), vmem_limit_bytes=VMEM_LIMIT),
        name="attention_t",
    )(qt, k, vt)


def _merge_kernel(x_ref, cb_ref, u_ref, uprev_ref, unext_ref, sga_ref, sgb_ref, ot_ref,
                  cw_ref, wco_ref, wao_ref, wo_ref, gt_ref, xo_ref):
    i = pl.program_id(1)
    last = pl.num_programs(1) - 1
    u = u_ref[...].astype(F32)
    tm = u.shape[0]
    halo = uprev_ref.shape[0]
    prev = jnp.where(i > 0, uprev_ref[halo - 1:halo, :].astype(F32), 0.0)
    nxt = jnp.where(i < last, unext_ref[0:1, :].astype(F32), 0.0)
    row = lax.broadcasted_iota(jnp.int32, u.shape, 0)
    u_m = jnp.where(row == 0, prev, pltpu.roll(u, 1, 0))
    u_p = jnp.where(row == tm - 1, nxt, pltpu.roll(u, tm - 1, 0))
    cw = cw_ref[...]
    conv = cw[0:1, :] * u_m + cw[1:2, :] * u + cw[2:3, :] * u_p
    za = (cb_ref[...].astype(F32) * conv).astype(BF16)
    ya = jnp.dot(za, wco_ref[...], preferred_element_type=F32)
    tn = (((0,), (0,)), ((), ()))
    yb = lax.dot_general(ot_ref[...], wao_ref[...], tn, preferred_element_type=F32)
    merged = (sga_ref[...].astype(F32) * ya + sgb_ref[...].astype(F32) * yb).astype(BF16)
    y = jnp.dot(merged, wo_ref[...], preferred_element_type=F32)
    xo_ref[...] = x_ref[...] + gt_ref[...] * y


def _merge_call(x, cb, u, sga, sgb, ot, gt1, wp):
    b, s, _ = x.shape
    tm = TOKEN_TILE
    halo = 16
    nh = tm // halo
    tile = lambda w: pl.BlockSpec((None, tm, w), lambda bi, i: (bi, i, 0))
    per_batch = pl.BlockSpec((None, 1, D_MODEL), lambda bi, i: (bi, 0, 0))
    in_specs = [
        tile(D_MODEL), tile(CONV_WIDTH), tile(CONV_WIDTH),
        pl.BlockSpec((None, halo, CONV_WIDTH), lambda bi, i: (bi, jnp.maximum(i * nh - 1, 0), 0)),
        pl.BlockSpec((None, halo, CONV_WIDTH), lambda bi, i: (bi, jnp.minimum((i + 1) * nh, s // halo - 1), 0)),
        tile(D_MODEL), tile(D_MODEL),
        pl.BlockSpec((None, N_HEADS * V_DIM, tm), lambda bi, i: (bi, 0, i)),
        _const_spec((3, CONV_WIDTH)), _const_spec((CONV_WIDTH, D_MODEL)),
        _const_spec((N_HEADS * V_DIM, D_MODEL)), _const_spec((D_MODEL, D_MODEL)), per_batch,
    ]
    return pl.pallas_call(
        _merge_kernel, out_shape=jax.ShapeDtypeStruct(x.shape, F32), grid=(b, s // tm),
        in_specs=in_specs, out_specs=tile(D_MODEL),
        compiler_params=pltpu.CompilerParams(
            dimension_semantics=("parallel", "parallel"), vmem_limit_bytes=VMEM_LIMIT),
        name="merge_out_proj",
    )(x, cb, u, u, u, sga, sgb, ot, wp["conv_w"], wp["w_conv_out"], wp["w_attn_out"], wp["w_o"], gt1)


def _route(logits_t, bias_col):
    scores = jax.nn.sigmoid(logits_t)
    biased = scores + bias_col
    neg_inf = jnp.float32(-jnp.inf)
    gscores = []
    for g in range(N_GROUPS):
        rws = [biased[g * GROUP_SIZE + j:g * GROUP_SIZE + j + 1] for j in range(GROUP_SIZE)]
        best = None
        for a in range(GROUP_SIZE):
            for bb in range(a + 1, GROUP_SIZE):
                pair = rws[a] + rws[bb]
                best = pair if best is None else jnp.maximum(best, pair)
        gscores.append(best)
    top = gscores[0]
    sel = jnp.zeros_like(top, dtype=jnp.int32)
    for g in range(1, N_GROUPS):
        better = gscores[g] > top
        sel = jnp.where(better, g, sel)
        top = jnp.where(better, gscores[g], top)
    eidx = lax.broadcasted_iota(jnp.int32, biased.shape, 0)
    masked = jnp.where(eidx // GROUP_SIZE == sel, biased, neg_inf)
    m1 = jnp.max(masked, axis=0, keepdims=True)
    i1 = jnp.min(jnp.where(masked == m1, eidx, N_EXPERTS), axis=0, keepdims=True)
    pick1 = eidx == i1
    masked2 = jnp.where(pick1, neg_inf, masked)
    m2 = jnp.max(masked2, axis=0, keepdims=True)
    i2 = jnp.min(jnp.where(masked2 == m2, eidx, N_EXPERTS), axis=0, keepdims=True)
    pick2 = eidx == i2
    s1 = jnp.sum(jnp.where(pick1, scores, 0.0), axis=0, keepdims=True)
    s2 = jnp.sum(jnp.where(pick2, scores, 0.0), axis=0, keepdims=True)
    return jnp.where(pick1 | pick2, scores / (s1 + s2), 0.0)


def _moe_kernel(x_ref, g_ref, sc_ref, sh_ref, gt_ref, wr_ref, br_ref,
                wgu_ref, wdn_ref, wsgu_ref, wsdn_ref, xo_ref, act_ref):
    x = x_ref[...]
    tm = x.shape[0]
    r = lax.rsqrt(jnp.mean(x * x, axis=-1, keepdims=True) + EPS)
    h = (x * r) * (g_ref[...] * (1.0 + sc_ref[...])) + sh_ref[...]
    hi = h.astype(BF16)
    lo = (h - hi.astype(F32)).astype(BF16)

    nt = (((1,), (1,)), ((), ()))
    l_hi = lax.dot_general(wr_ref[...], hi, nt, preferred_element_type=F32)
    l_lo = lax.dot_general(wr_ref[0:N_EXPERTS, :], lo, nt, preferred_element_type=F32)
    logits_t = l_hi[0:N_EXPERTS] + l_hi[N_EXPERTS:2 * N_EXPERTS] + l_lo
    comb_t = _route(logits_t, br_ref[...])
    comb = jnp.concatenate([comb_t, jnp.zeros((128 - N_EXPERTS, tm), F32)], axis=0).T

    for e in range(N_EXPERTS):
        gu = jnp.dot(hi, wgu_ref[e], preferred_element_type=F32)
        a = _silu(gu[:, 0:D_EXPERT]) * gu[:, D_EXPERT:2 * D_EXPERT] * comb[:, e:e + 1]
        act_ref[:, e * D_EXPERT:(e + 1) * D_EXPERT] = a.astype(BF16)
    gs = jnp.dot(hi, wsgu_ref[...], preferred_element_type=F32)
    a_s = _silu(gs[:, 0:D_SHARED]) * gs[:, D_SHARED:2 * D_SHARED]
    y = jnp.dot(act_ref[...], wdn_ref[...], preferred_element_type=F32)
    y = y + jnp.dot(a_s.astype(BF16), wsdn_ref[...], preferred_element_type=F32)
    xo_ref[...] = x + gt_ref[...] * y


def _moe_call(x, g_ffn, sc2, sh2, gt2, wp, wr):
    b, s, _ = x.shape
    tm = TOKEN_TILE
    tile = pl.BlockSpec((None, tm, D_MODEL), lambda bi, i: (bi, i, 0))
    per_batch = pl.BlockSpec((None, 1, D_MODEL), lambda bi, i: (bi, 0, 0))
    in_specs = [
        tile, _const_spec((1, D_MODEL)), per_batch, per_batch, per_batch,
        _const_spec((2 * N_EXPERTS, D_MODEL)), _const_spec((N_EXPERTS, 1)),
        _const_spec((N_EXPERTS, D_MODEL, 2 * D_EXPERT)), _const_spec((N_EXPERTS * D_EXPERT, D_MODEL)),
        _const_spec((D_MODEL, 2 * D_SHARED)), _const_spec((D_SHARED, D_MODEL)),
    ]
    return pl.pallas_call(
        _moe_kernel, out_shape=jax.ShapeDtypeStruct(x.shape, F32), grid=(b, s // tm),
        in_specs=in_specs, out_specs=tile,
        scratch_shapes=[pltpu.VMEM((tm, N_EXPERTS * D_EXPERT), BF16)],
        compiler_params=pltpu.CompilerParams(
            dimension_semantics=("parallel", "parallel"), vmem_limit_bytes=VMEM_LIMIT),
        name="moe_ffn",
    )(x, g_ffn, sc2, sh2, gt2, wr["w_hl"], wr["b"], wp["w_gu"], wp["w_dn"],
      wp["w_s_gu"], wp["w_s_dn"])


def _prep_layer(l, w_in, conv_w, g_q_a, w_uq, g_kv_a, w_ukv, g_qn, g_kn, w_conv_out, w_attn_out,
                w_o, w_e_gate, w_e_up, w_e_down, w_s_gate, w_s_up, w_s_down):
    wi = w_in[l]
    split = 3 * CONV_WIDTH + Q_RANK + KV_RANK
    kpe_block = jnp.zeros((D_MODEL, HEAD_PAD), F32).at[:, QK_NOPE:QK_DIM].set(wi[:, split:split + QK_ROPE])
    w_in_r = jnp.concatenate([wi[:, :split], kpe_block, wi[:, split + QK_ROPE:]], axis=1).astype(BF16)

    wq = w_uq[l].reshape(Q_RANK, N_HEADS, QK_DIM)
    wq = jnp.pad(wq, ((0, 0), (0, 0), (0, HEAD_PAD - QK_DIM))).reshape(Q_RANK, N_HEADS * HEAD_PAD)
    wkv = w_ukv[l].reshape(KV_RANK, N_HEADS, QK_NOPE + V_DIM)
    wk = jnp.pad(wkv[:, :, :QK_NOPE], ((0, 0), (0, 0), (0, HEAD_PAD - QK_NOPE))).reshape(KV_RANK, N_HEADS * HEAD_PAD)
    wv = wkv[:, :, QK_NOPE:].reshape(KV_RANK, N_HEADS * V_DIM)
    scale = QK_DIM ** -0.5
    g_q = jnp.pad(g_qn[l] * scale, (0, HEAD_PAD - QK_DIM)).reshape(HEAD_PAD, 1)
    g_k = jnp.pad(g_kn[l], (0, HEAD_PAD - QK_DIM)).reshape(1, HEAD_PAD)
    return {
        "w_in": w_in_r,
        "g_q_a": g_q_a[l].reshape(1, Q_RANK),
        "w_uq_t": wq.T.astype(BF16),
        "g_kv_a": g_kv_a[l].reshape(1, KV_RANK),
        "w_uk": wk.astype(BF16),
        "w_uv_t": wv.T.astype(BF16),
        "g_q": g_q, "g_k": g_k,
        "conv_w": conv_w[l],
        "w_conv_out": w_conv_out[l].astype(BF16),
        "w_attn_out": w_attn_out[l].astype(BF16),
        "w_o": w_o[l].astype(BF16),
        "w_gu": jnp.concatenate([w_e_gate[l], w_e_up[l]], axis=-1).astype(BF16),
        "w_dn": w_e_down[l].reshape(N_EXPERTS * D_EXPERT, D_MODEL).astype(BF16),
        "w_s_gu": jnp.concatenate([w_s_gate[l], w_s_up[l]], axis=-1).astype(BF16),
        "w_s_dn": w_s_down[l].astype(BF16),
    }


def _rope_tables(positions):
    b, s = positions.shape
    inv_freq = ROPE_THETA ** (-jnp.arange(ROPE_HALF, dtype=F32) / ROPE_HALF)
    ang = positions.astype(F32)[..., None] * inv_freq
    cos = jnp.cos(ang)
    sin = jnp.sin(ang)
    zeros16 = jnp.zeros_like(cos)
    ones64 = jnp.ones((b, s, QK_NOPE), F32)
    zeros32 = jnp.zeros((b, s, HEAD_PAD - QK_DIM), F32)
    zeros64 = jnp.zeros((b, s, QK_NOPE), F32)
    return {
        "cos_t": cos.transpose(0, 2, 1), "sin_t": sin.transpose(0, 2, 1),
        "ck": jnp.concatenate([ones64, cos, cos, zeros32], axis=-1),
        "sa": jnp.concatenate([zeros64, -sin, zeros16, zeros32], axis=-1),
        "sb": jnp.concatenate([zeros64, zeros16, sin, zeros32], axis=-1),
    }


def kernel(x, c, positions, w_ada, b_ada, g_mix, g_ffn, w_in, conv_w, g_q_a, w_uq, g_kv_a, w_ukv,
           g_qn, g_kn, w_conv_out, w_attn_out, w_o, w_router, b_router, w_e_gate, w_e_up, w_e_down,
           w_s_gate, w_s_up, w_s_down):
    depth = w_ada.shape[0]
    b = x.shape[0]
    c_pad = jnp.pad(c, ((0, 8 - b), (0, 0)))
    mod = _modulation(c_pad, w_ada, b_ada)[:, :b, :]
    mod = mod.reshape(depth, b, 6, 1, D_MODEL)

    tabs = _rope_tables(positions)
    wr_t = w_router.T
    wr_hi = wr_t.astype(BF16)
    wr_lo = (wr_t - wr_hi.astype(F32)).astype(BF16)
    wr = {"w_hl": jnp.concatenate([wr_hi, wr_lo], axis=0), "b": b_router.reshape(N_EXPERTS, 1)}

    for l in range(depth):
        wp = _prep_layer(l, w_in, conv_w, g_q_a, w_uq, g_kv_a, w_ukv, g_qn, g_kn, w_conv_out,
                         w_attn_out, w_o, w_e_gate, w_e_up, w_e_down, w_s_gate, w_s_up, w_s_down)
        sh1, sc1, gt1, sh2, sc2, gt2 = (mod[l, :, j] for j in range(6))
        cb, u, sga, sgb, qt, k, vt = _proj_call(x, g_mix[l].reshape(1, D_MODEL), sc1, sh1, wp, tabs)
        ot = _attn_call(qt, k, vt)
        x = _merge_call(x, cb, u, sga, sgb, ot, gt1, wp)
        x = _moe_call(x, g_ffn[l].reshape(1, D_MODEL), sc2, sh2, gt2, wp, wr)
    return x
```

```python
import functools

import jax
import jax.numpy as jnp
from jax import lax
from jax.experimental import pallas as pl
from jax.experimental.pallas import tpu as pltpu

F32 = jnp.float32
BF16 = jnp.bfloat16

D_MODEL = 1024
CONV_WIDTH = 512
N_HEADS = 8
QK_NOPE = 64
QK_ROPE = 32
V_DIM = 64
QK_DIM = QK_NOPE + QK_ROPE
Q_RANK = 256
KV_RANK = 128
ROPE_THETA = 10000.0
N_EXPERTS = 16
N_GROUPS = 4
GROUP_SIZE = N_EXPERTS // N_GROUPS
D_EXPERT = 256
D_SHARED = 256
EPS = 1e-6

HEAD_PAD = 128
V_ROWS = 80
LAT_START = 3 * CONV_WIDTH
GATE_START = LAT_START + Q_RANK + KV_RANK + 2 * HEAD_PAD
IN_PAD = GATE_START + 2 * D_MODEL
ROPE_HALF = QK_ROPE // 2

TOKEN_TILE = 512
K_CHUNK = 512
STEP_CHUNKS = 2
PIPE_LAG = 2
BUF_SETS = 2 * PIPE_LAG
LOG2_E = 1.4426950408889634
VMEM_LIMIT = 56 * 1024 * 1024
NEG_BIG = -1e30


def _const_spec(shape):
    nd = len(shape)
    return pl.BlockSpec(shape, lambda *_: (0,) * nd, pipeline_mode=pl.Buffered(1))


def _silu(v):
    return v * jax.nn.sigmoid(v)


def _mod_kernel(c_ref, w_ref, b_ref, o_ref):
    c = c_ref[...]
    o_ref[...] = jnp.dot(_silu(c), w_ref[...], precision=lax.Precision.HIGHEST,
                         preferred_element_type=F32) + b_ref[...]


def _modulation(c_pad, w_ada, b_ada):
    depth = w_ada.shape[0]
    nblk = w_ada.shape[2] // D_MODEL
    rows = c_pad.shape[0]
    return pl.pallas_call(
        _mod_kernel,
        out_shape=jax.ShapeDtypeStruct((depth, rows, w_ada.shape[2]), F32),
        grid=(depth, nblk),
        in_specs=[
            pl.BlockSpec((rows, D_MODEL), lambda l, j: (0, 0)),
            pl.BlockSpec((None, D_MODEL, D_MODEL), lambda l, j: (l, 0, j)),
            pl.BlockSpec((None, 1, D_MODEL), lambda l, j: (l, 0, j)),
        ],
        out_specs=pl.BlockSpec((None, rows, D_MODEL), lambda l, j: (l, 0, j)),
        compiler_params=pltpu.CompilerParams(
            dimension_semantics=("arbitrary", "arbitrary"), vmem_limit_bytes=VMEM_LIMIT),
        name="adaln_mod",
    )(c_pad, w_ada, b_ada.reshape(depth, 1, -1))


def _proj_kernel(x_ref, g_ref, sc_ref, sh_ref, win_ref, gqa_ref, wuqt_ref, gkva_ref, wuk_ref,
                 wuvt_ref, gq_ref, gk_ref, gksw_ref, ones_ref, cost_ref, sint_ref, ck_ref, ss_ref,
                 cb_ref, u_ref, sga_ref, sgb_ref, qt_ref, k_ref, vt_ref):
    x = x_ref[...]
    r = lax.rsqrt(jnp.mean(x * x, axis=-1, keepdims=True) + EPS)
    h = (x * r) * (g_ref[...] * (1.0 + sc_ref[...])) + sh_ref[...]
    hb = h.astype(BF16)
    tm = x.shape[0]

    pc = jnp.dot(hb, win_ref[:, 0:3 * CONV_WIDTH], preferred_element_type=F32)
    cb_ref[...] = pc[:, 0:CONV_WIDTH].astype(BF16)
    u_ref[...] = (pc[:, CONV_WIDTH:2 * CONV_WIDTH] * pc[:, 2 * CONV_WIDTH:3 * CONV_WIDTH]).astype(BF16)

    pg = jnp.dot(hb, win_ref[:, GATE_START:IN_PAD], preferred_element_type=F32)
    sga_ref[...] = jax.nn.sigmoid(pg[:, 0:D_MODEL]).astype(BF16)
    sgb_ref[...] = jax.nn.sigmoid(pg[:, D_MODEL:2 * D_MODEL]).astype(BF16)

    lat = jnp.dot(hb, win_ref[:, LAT_START:GATE_START], preferred_element_type=F32)
    cq = lat[:, 0:Q_RANK]
    ckv = lat[:, Q_RANK:Q_RANK + KV_RANK]
    kpe = lat[:, Q_RANK + KV_RANK:Q_RANK + KV_RANK + HEAD_PAD]
    kpe_sw = lat[:, Q_RANK + KV_RANK + HEAD_PAD:Q_RANK + KV_RANK + 2 * HEAD_PAD]

    cqn = (cq * lax.rsqrt(jnp.mean(cq * cq, axis=-1, keepdims=True) + EPS) * gqa_ref[...]).astype(BF16)
    ckvn = (ckv * lax.rsqrt(jnp.mean(ckv * ckv, axis=-1, keepdims=True) + EPS) * gkva_ref[...]).astype(BF16)

    nt = (((1,), (1,)), ((), ()))
    qt = lax.dot_general(wuqt_ref[...], cqn, nt, preferred_element_type=F32)
    cost = cost_ref[...]
    sint = sint_ref[...]
    gq = gq_ref[...]
    for hh in range(N_HEADS):
        base = hh * HEAD_PAD
        qh = qt[base:base + HEAD_PAD]
        rr = lax.rsqrt(jnp.sum(qh * qh, axis=0, keepdims=True) * (1.0 / QK_DIM) + EPS)
        qn = qh * rr * gq
        x1 = qn[QK_NOPE:QK_NOPE + ROPE_HALF]
        x2 = qn[QK_NOPE + ROPE_HALF:QK_DIM]
        qt_ref[base:base + QK_NOPE, :] = qn[0:QK_NOPE].astype(BF16)
        qt_ref[base + QK_NOPE:base + QK_NOPE + ROPE_HALF, :] = (x1 * cost - x2 * sint).astype(BF16)
        qt_ref[base + QK_NOPE + ROPE_HALF:base + QK_DIM, :] = (x2 * cost + x1 * sint).astype(BF16)
        qt_ref[base + QK_DIM:base + HEAD_PAD, :] = jnp.zeros((HEAD_PAD - QK_DIM, tm), BF16)

    kn = jnp.dot(ckvn, wuk_ref[...], preferred_element_type=F32)
    direct = gk_ref[...] * ck_ref[...]
    partner = kpe_sw * (gksw_ref[...] * ss_ref[...])
    ones = ones_ref[...]
    for hh in range(N_HEADS):
        base = hh * HEAD_PAD
        kh = kn[:, base:base + HEAD_PAD] + kpe
        ssq = jnp.dot((kh * kh).astype(BF16), ones, preferred_element_type=F32)
        rr = lax.rsqrt(ssq * (1.0 / QK_DIM) + EPS)
        k_ref[:, base:base + HEAD_PAD] = (rr * (kh * direct + partner)).astype(BF16)

    vt = lax.dot_general(wuvt_ref[...], ckvn, nt, preferred_element_type=F32)
    for hh in range(N_HEADS):
        vt_ref[hh * V_ROWS:hh * V_ROWS + V_DIM, :] = vt[hh * V_DIM:(hh + 1) * V_DIM].astype(BF16)
        vt_ref[hh * V_ROWS + V_DIM:(hh + 1) * V_ROWS, :] = jnp.ones((V_ROWS - V_DIM, tm), BF16)


def _proj_call(x, g_mix, sc1, sh1, wp, tabs):
    b, s, _ = x.shape
    tm = TOKEN_TILE
    tile = lambda w: pl.BlockSpec((None, tm, w), lambda bi, i: (bi, i, 0))
    tile_t = lambda rws: pl.BlockSpec((None, rws, tm), lambda bi, i: (bi, 0, i))
    per_batch = pl.BlockSpec((None, 1, D_MODEL), lambda bi, i: (bi, 0, 0))
    in_specs = [
        tile(D_MODEL), _const_spec((1, D_MODEL)), per_batch, per_batch,
        _const_spec((D_MODEL, IN_PAD)), _const_spec((1, Q_RANK)),
        _const_spec((N_HEADS * HEAD_PAD, Q_RANK)), _const_spec((1, KV_RANK)),
        _const_spec((KV_RANK, N_HEADS * HEAD_PAD)), _const_spec((N_HEADS * V_DIM, KV_RANK)),
        _const_spec((HEAD_PAD, 1)), _const_spec((1, HEAD_PAD)), _const_spec((1, HEAD_PAD)),
        _const_spec((HEAD_PAD, HEAD_PAD)),
        tile_t(ROPE_HALF), tile_t(ROPE_HALF), tile(HEAD_PAD), tile(HEAD_PAD),
    ]
    out_shape = [
        jax.ShapeDtypeStruct((b, s, CONV_WIDTH), BF16),
        jax.ShapeDtypeStruct((b, s, CONV_WIDTH), BF16),
        jax.ShapeDtypeStruct((b, s, D_MODEL), BF16),
        jax.ShapeDtypeStruct((b, s, D_MODEL), BF16),
        jax.ShapeDtypeStruct((b, s // tm, N_HEADS * HEAD_PAD, tm), BF16),
        jax.ShapeDtypeStruct((b, s, N_HEADS * HEAD_PAD), BF16),
        jax.ShapeDtypeStruct((b, N_HEADS * V_ROWS, s), BF16),
    ]
    out_specs = [tile(CONV_WIDTH), tile(CONV_WIDTH), tile(D_MODEL), tile(D_MODEL),
                 pl.BlockSpec((None, None, N_HEADS * HEAD_PAD, tm), lambda bi, i: (bi, i, 0, 0)),
                 tile(N_HEADS * HEAD_PAD), tile_t(N_HEADS * V_ROWS)]
    return pl.pallas_call(
        _proj_kernel, out_shape=out_shape, grid=(b, s // tm),
        in_specs=in_specs, out_specs=out_specs,
        compiler_params=pltpu.CompilerParams(
            dimension_semantics=("parallel", "parallel"), vmem_limit_bytes=VMEM_LIMIT),
        name="norm_in_proj",
    )(x, g_mix, sc1, sh1, wp["w_in"], wp["g_q_a"], wp["w_uq_t"], wp["g_kv_a"], wp["w_uk"],
      wp["w_uv_t"], wp["g_q"], wp["g_k"], wp["g_k_sw"], jnp.ones((HEAD_PAD, HEAD_PAD), BF16),
      tabs["cos_t"], tabs["sin_t"], tabs["ck"], tabs["ss"])


def _attn_kernel(qt_ref, k_ref, vt_ref, ot_ref, *scratch):
    s_bufs = [scratch[i * STEP_CHUNKS:(i + 1) * STEP_CHUNKS] for i in range(BUF_SETS)]
    p_bufs = [scratch[(BUF_SETS + i) * STEP_CHUNKS:(BUF_SETS + i + 1) * STEP_CHUNKS] for i in range(BUF_SETS)]
    nq, _, tq = qt_ref.shape
    tk = K_CHUNK
    steps_per_tile = k_ref.shape[0] // (tk * STEP_CHUNKS)
    nsteps = nq * steps_per_tile

    def key_off(g, c):
        return pl.multiple_of(((g % steps_per_tile) * STEP_CHUNKS + c) * tk, tk)

    def scores(g, bset):
        g = jnp.minimum(g, nsteps - 1)
        qt = qt_ref[g // steps_per_tile]
        col_max = []
        for c in range(STEP_CHUNKS):
            sc = jnp.dot(k_ref[pl.ds(key_off(g, c), tk), :], qt, preferred_element_type=F32)
            s_bufs[bset][c][...] = sc
            col_max.append(jnp.max(sc, axis=0, keepdims=True))
        return col_max

    def exponentials(g, bset, m, col_max):
        m = jnp.where(g % steps_per_tile == 0, NEG_BIG, m)
        alphas = []
        for c in range(STEP_CHUNKS):
            m_new = jnp.maximum(m, col_max[c])
            alphas.append(jnp.exp2(m - m_new))
            p_bufs[bset][c][...] = jnp.exp2(s_bufs[bset][c][...] - m_new).astype(BF16)
            m = m_new
        return m, alphas

    def values(g, bset, acc, alphas):
        g = jnp.maximum(g, 0)
        for c in range(STEP_CHUNKS):
            vc = vt_ref[:, pl.ds(key_off(g, c), tk)]
            acc = alphas[c] * acc + jnp.dot(vc, p_bufs[bset][c][...], preferred_element_type=F32)
        ot_ref[g // steps_per_tile] = (acc[0:V_DIM] / acc[V_DIM:V_DIM + 1]).astype(BF16)
        return acc

    def step(g, bset, carry):
        m, acc, alpha_old, alpha_new, cmax_old, cmax_new = carry
        cmax = scores(g + PIPE_LAG, (bset + PIPE_LAG) % BUF_SETS)
        m, alpha = exponentials(g, bset, m, cmax_old)
        acc = values(g - PIPE_LAG, (bset - PIPE_LAG) % BUF_SETS, acc, alpha_old)
        return m, acc, alpha_new, alpha, cmax_new, cmax

    def trip(t, carry):
        for bset in range(BUF_SETS):
            carry = step(BUF_SETS * t + bset, bset, carry)
        return carry

    for bset in range(BUF_SETS - PIPE_LAG, BUF_SETS):
        for c in range(STEP_CHUNKS):
            p_bufs[bset][c][...] = jnp.zeros((tk, tq), BF16)
    m0 = jnp.full((1, tq), NEG_BIG, F32)
    acc0 = jnp.concatenate([jnp.zeros((V_DIM, tq), F32), jnp.ones((V_ROWS - V_DIM, tq), F32)], axis=0)
    ones = [jnp.ones((1, tq), F32)] * STEP_CHUNKS
    cmax0 = scores(0, 0)
    cmax1 = scores(1, 1)
    _, acc, alpha_old, alpha_new, _, _ = lax.fori_loop(
        0, nsteps // BUF_SETS, trip, (m0, acc0, ones, ones, cmax0, cmax1))
    acc = values(nsteps - 2, (nsteps - 2) % BUF_SETS, acc, alpha_old)
    values(nsteps - 1, (nsteps - 1) % BUF_SETS, acc, alpha_new)


def _attn_call(qt, k, vt):
    b, nq, _, tq = qt.shape
    s = k.shape[1]
    assert PIPE_LAG == 2 and BUF_SETS == 2 * PIPE_LAG
    assert (nq * (s // (K_CHUNK * STEP_CHUNKS))) % BUF_SETS == 0
    nbuf = BUF_SETS * STEP_CHUNKS
    return pl.pallas_call(
        _attn_kernel,
        out_shape=jax.ShapeDtypeStruct((b, nq, N_HEADS * V_DIM, tq), BF16),
        grid=(b, N_HEADS),
        in_specs=[
            pl.BlockSpec((None, nq, HEAD_PAD, tq), lambda bi, hh: (bi, 0, hh, 0)),
            pl.BlockSpec((None, s, HEAD_PAD), lambda bi, hh: (bi, 0, hh)),
            pl.BlockSpec((None, V_ROWS, s), lambda bi, hh: (bi, hh, 0)),
        ],
        out_specs=pl.BlockSpec((None, nq, V_DIM, tq), lambda bi, hh: (bi, 0, hh, 0)),
        scratch_shapes=[pltpu.VMEM((K_CHUNK, tq), F32)] * nbuf + [pltpu.VMEM((K_CHUNK, tq), BF16)] * nbuf,
        compiler_params=pltpu.CompilerParams(
            dimension_semantics=("parallel", "parallel"), vmem_limit_bytes=VMEM_LIMIT),
        name="attention_t",
    )(qt, k, vt)


def _merge_kernel(x_ref, cb_ref, u_ref, uprev_ref, unext_ref, sga_ref, sgb_ref, ot_ref,
                  cw_ref, wco_ref, wao_ref, wo_ref, gt_ref, xo_ref):
    i = pl.program_id(1)
    last = pl.num_programs(1) - 1
    u = u_ref[...].astype(F32)
    tm = u.shape[0]
    halo = uprev_ref.shape[0]
    prev = jnp.where(i > 0, uprev_ref[halo - 1:halo, :].astype(F32), 0.0)
    nxt = jnp.where(i < last, unext_ref[0:1, :].astype(F32), 0.0)
    row = lax.broadcasted_iota(jnp.int32, u.shape, 0)
    u_m = jnp.where(row == 0, prev, pltpu.roll(u, 1, 0))
    u_p = jnp.where(row == tm - 1, nxt, pltpu.roll(u, tm - 1, 0))
    cw = cw_ref[...]
    conv = cw[0:1, :] * u_m + cw[1:2, :] * u + cw[2:3, :] * u_p
    za = (cb_ref[...].astype(F32) * conv).astype(BF16)
    ya = jnp.dot(za, wco_ref[...], preferred_element_type=F32)
    tn = (((0,), (0,)), ((), ()))
    yb = lax.dot_general(ot_ref[...], wao_ref[...], tn, preferred_element_type=F32)
    merged = (sga_ref[...].astype(F32) * ya + sgb_ref[...].astype(F32) * yb).astype(BF16)
    y = jnp.dot(merged, wo_ref[...], preferred_element_type=F32)
    xo_ref[...] = x_ref[...] + gt_ref[...] * y


def _merge_call(x, cb, u, sga, sgb, ot, gt1, wp):
    b, s, _ = x.shape
    tm = TOKEN_TILE
    halo = 16
    nh = tm // halo
    tile = lambda w: pl.BlockSpec((None, tm, w), lambda bi, i: (bi, i, 0))
    per_batch = pl.BlockSpec((None, 1, D_MODEL), lambda bi, i: (bi, 0, 0))
    in_specs = [
        tile(D_MODEL), tile(CONV_WIDTH), tile(CONV_WIDTH),
        pl.BlockSpec((None, halo, CONV_WIDTH), lambda bi, i: (bi, jnp.maximum(i * nh - 1, 0), 0)),
        pl.BlockSpec((None, halo, CONV_WIDTH), lambda bi, i: (bi, jnp.minimum((i + 1) * nh, s // halo - 1), 0)),
        tile(D_MODEL), tile(D_MODEL),
        pl.BlockSpec((None, None, N_HEADS * V_DIM, tm), lambda bi, i: (bi, i, 0, 0)),
        _const_spec((3, CONV_WIDTH)), _const_spec((CONV_WIDTH, D_MODEL)),
        _const_spec((N_HEADS * V_DIM, D_MODEL)), _const_spec((D_MODEL, D_MODEL)), per_batch,
    ]
    return pl.pallas_call(
        _merge_kernel, out_shape=jax.ShapeDtypeStruct(x.shape, F32), grid=(b, s // tm),
        in_specs=in_specs, out_specs=tile(D_MODEL),
        compiler_params=pltpu.CompilerParams(
            dimension_semantics=("parallel", "parallel"), vmem_limit_bytes=VMEM_LIMIT),
        name="merge_out_proj",
    )(x, cb, u, u, u, sga, sgb, ot, wp["conv_w"], wp["w_conv_out"], wp["w_attn_out"], wp["w_o"], gt1)


def _route(logits_t, bias_col):
    scores = jax.nn.sigmoid(logits_t)
    biased = scores + bias_col
    neg_inf = jnp.float32(-jnp.inf)
    gscores = []
    for g in range(N_GROUPS):
        rws = [biased[g * GROUP_SIZE + j:g * GROUP_SIZE + j + 1] for j in range(GROUP_SIZE)]
        best = None
        for a in range(GROUP_SIZE):
            for bb in range(a + 1, GROUP_SIZE):
                pair = rws[a] + rws[bb]
                best = pair if best is None else jnp.maximum(best, pair)
        gscores.append(best)
    top = gscores[0]
    sel = jnp.zeros_like(top, dtype=jnp.int32)
    for g in range(1, N_GROUPS):
        better = gscores[g] > top
        sel = jnp.where(better, g, sel)
        top = jnp.where(better, gscores[g], top)
    eidx = lax.broadcasted_iota(jnp.int32, biased.shape, 0)
    masked = jnp.where(eidx // GROUP_SIZE == sel, biased, neg_inf)
    m1 = jnp.max(masked, axis=0, keepdims=True)
    i1 = jnp.min(jnp.where(masked == m1, eidx, N_EXPERTS), axis=0, keepdims=True)
    pick1 = eidx == i1
    masked2 = jnp.where(pick1, neg_inf, masked)
    m2 = jnp.max(masked2, axis=0, keepdims=True)
    i2 = jnp.min(jnp.where(masked2 == m2, eidx, N_EXPERTS), axis=0, keepdims=True)
    pick2 = eidx == i2
    s1 = jnp.sum(jnp.where(pick1, scores, 0.0), axis=0, keepdims=True)
    s2 = jnp.sum(jnp.where(pick2, scores, 0.0), axis=0, keepdims=True)
    return jnp.where(pick1 | pick2, scores / (s1 + s2), 0.0)


def _moe_kernel(x_ref, g_ref, sc_ref, sh_ref, gt_ref, wr_ref, br_ref,
                wgu_ref, wdn_ref, wsgu_ref, wsdn_ref, xo_ref, act_ref):
    x = x_ref[...]
    tm = x.shape[0]
    r = lax.rsqrt(jnp.mean(x * x, axis=-1, keepdims=True) + EPS)
    h = (x * r) * (g_ref[...] * (1.0 + sc_ref[...])) + sh_ref[...]
    hi = h.astype(BF16)
    lo = (h - hi.astype(F32)).astype(BF16)

    nt = (((1,), (1,)), ((), ()))
    l_hi = lax.dot_general(wr_ref[...], hi, nt, preferred_element_type=F32)
    l_lo = lax.dot_general(wr_ref[0:N_EXPERTS, :], lo, nt, preferred_element_type=F32)
    logits_t = l_hi[0:N_EXPERTS] + l_hi[N_EXPERTS:2 * N_EXPERTS] + l_lo
    comb_t = _route(logits_t, br_ref[...])
    comb = jnp.concatenate([comb_t, jnp.zeros((128 - N_EXPERTS, tm), F32)], axis=0).T

    for e in range(N_EXPERTS):
        gu = jnp.dot(hi, wgu_ref[e], preferred_element_type=F32)
        a = _silu(gu[:, 0:D_EXPERT]) * gu[:, D_EXPERT:2 * D_EXPERT] * comb[:, e:e + 1]
        act_ref[:, e * D_EXPERT:(e + 1) * D_EXPERT] = a.astype(BF16)
    gs = jnp.dot(hi, wsgu_ref[...], preferred_element_type=F32)
    a_s = _silu(gs[:, 0:D_SHARED]) * gs[:, D_SHARED:2 * D_SHARED]
    y = jnp.dot(act_ref[...], wdn_ref[...], preferred_element_type=F32)
    y = y + jnp.dot(a_s.astype(BF16), wsdn_ref[...], preferred_element_type=F32)
    xo_ref[...] = x + gt_ref[...] * y


def _moe_call(x, g_ffn, sc2, sh2, gt2, wp, wr):
    b, s, _ = x.shape
    tm = TOKEN_TILE
    tile = pl.BlockSpec((None, tm, D_MODEL), lambda bi, i: (bi, i, 0))
    per_batch = pl.BlockSpec((None, 1, D_MODEL), lambda bi, i: (bi, 0, 0))
    in_specs = [
        tile, _const_spec((1, D_MODEL)), per_batch, per_batch, per_batch,
        _const_spec((2 * N_EXPERTS, D_MODEL)), _const_spec((N_EXPERTS, 1)),
        _const_spec((N_EXPERTS, D_MODEL, 2 * D_EXPERT)), _const_spec((N_EXPERTS * D_EXPERT, D_MODEL)),
        _const_spec((D_MODEL, 2 * D_SHARED)), _const_spec((D_SHARED, D_MODEL)),
    ]
    return pl.pallas_call(
        _moe_kernel, out_shape=jax.ShapeDtypeStruct(x.shape, F32), grid=(b, s // tm),
        in_specs=in_specs, out_specs=tile,
        scratch_shapes=[pltpu.VMEM((tm, N_EXPERTS * D_EXPERT), BF16)],
        compiler_params=pltpu.CompilerParams(
            dimension_semantics=("parallel", "parallel"), vmem_limit_bytes=VMEM_LIMIT),
        name="moe_ffn",
    )(x, g_ffn, sc2, sh2, gt2, wr["w_hl"], wr["b"], wp["w_gu"], wp["w_dn"],
      wp["w_s_gu"], wp["w_s_dn"])


def _prep_layer(l, w_in, conv_w, g_q_a, w_uq, g_kv_a, w_ukv, g_qn, g_kn, w_conv_out, w_attn_out,
                w_o, w_e_gate, w_e_up, w_e_down, w_s_gate, w_s_up, w_s_down):
    wi = w_in[l]
    split = 3 * CONV_WIDTH + Q_RANK + KV_RANK
    w_pe = wi[:, split:split + QK_ROPE]
    w_pe_sw = jnp.concatenate([w_pe[:, ROPE_HALF:], w_pe[:, :ROPE_HALF]], axis=1)
    lane_pad = lambda w: jnp.pad(w, ((0, 0), (QK_NOPE, HEAD_PAD - QK_DIM)))
    w_in_r = jnp.concatenate(
        [wi[:, :split], lane_pad(w_pe), lane_pad(w_pe_sw), wi[:, split + QK_ROPE:]], axis=1).astype(BF16)

    wq = w_uq[l].reshape(Q_RANK, N_HEADS, QK_DIM)
    wq = jnp.pad(wq, ((0, 0), (0, 0), (0, HEAD_PAD - QK_DIM))).reshape(Q_RANK, N_HEADS * HEAD_PAD)
    wkv = w_ukv[l].reshape(KV_RANK, N_HEADS, QK_NOPE + V_DIM)
    wk = jnp.pad(wkv[:, :, :QK_NOPE], ((0, 0), (0, 0), (0, HEAD_PAD - QK_NOPE))).reshape(KV_RANK, N_HEADS * HEAD_PAD)
    wv = wkv[:, :, QK_NOPE:].reshape(KV_RANK, N_HEADS * V_DIM)
    scale = QK_DIM ** -0.5 * LOG2_E
    g_q = jnp.pad(g_qn[l] * scale, (0, HEAD_PAD - QK_DIM)).reshape(HEAD_PAD, 1)
    g_k = jnp.pad(g_kn[l], (0, HEAD_PAD - QK_DIM)).reshape(1, HEAD_PAD)
    g_rope = g_kn[l][QK_NOPE:]
    g_k_sw = jnp.pad(jnp.concatenate([g_rope[ROPE_HALF:], g_rope[:ROPE_HALF]]),
                     (QK_NOPE, HEAD_PAD - QK_DIM)).reshape(1, HEAD_PAD)
    return {
        "w_in": w_in_r,
        "g_q_a": g_q_a[l].reshape(1, Q_RANK),
        "w_uq_t": wq.T.astype(BF16),
        "g_kv_a": g_kv_a[l].reshape(1, KV_RANK),
        "w_uk": wk.astype(BF16),
        "w_uv_t": wv.T.astype(BF16),
        "g_q": g_q, "g_k": g_k, "g_k_sw": g_k_sw,
        "conv_w": conv_w[l],
        "w_conv_out": w_conv_out[l].astype(BF16),
        "w_attn_out": w_attn_out[l].astype(BF16),
        "w_o": w_o[l].astype(BF16),
        "w_gu": jnp.concatenate([w_e_gate[l], w_e_up[l]], axis=-1).astype(BF16),
        "w_dn": w_e_down[l].reshape(N_EXPERTS * D_EXPERT, D_MODEL).astype(BF16),
        "w_s_gu": jnp.concatenate([w_s_gate[l], w_s_up[l]], axis=-1).astype(BF16),
        "w_s_dn": w_s_down[l].astype(BF16),
    }


def _rope_tables(positions):
    b, s = positions.shape
    inv_freq = ROPE_THETA ** (-jnp.arange(ROPE_HALF, dtype=F32) / ROPE_HALF)
    ang = positions.astype(F32)[..., None] * inv_freq
    cos = jnp.cos(ang)
    sin = jnp.sin(ang)
    ones64 = jnp.ones((b, s, QK_NOPE), F32)
    zeros32 = jnp.zeros((b, s, HEAD_PAD - QK_DIM), F32)
    zeros64 = jnp.zeros((b, s, QK_NOPE), F32)
    return {
        "cos_t": cos.transpose(0, 2, 1), "sin_t": sin.transpose(0, 2, 1),
        "ck": jnp.concatenate([ones64, cos, cos, zeros32], axis=-1),
        "ss": jnp.concatenate([zeros64, -sin, sin, zeros32], axis=-1),
    }


def kernel(x, c, positions, w_ada, b_ada, g_mix, g_ffn, w_in, conv_w, g_q_a, w_uq, g_kv_a, w_ukv,
           g_qn, g_kn, w_conv_out, w_attn_out, w_o, w_router, b_router, w_e_gate, w_e_up, w_e_down,
           w_s_gate, w_s_up, w_s_down):
    depth = w_ada.shape[0]
    b = x.shape[0]
    c_pad = jnp.pad(c, ((0, 8 - b), (0, 0)))
    mod = _modulation(c_pad, w_ada, b_ada)[:, :b, :]
    mod = mod.reshape(depth, b, 6, 1, D_MODEL)

    tabs = _rope_tables(positions)
    wr_t = w_router.T
    wr_hi = wr_t.astype(BF16)
    wr_lo = (wr_t - wr_hi.astype(F32)).astype(BF16)
    wr = {"w_hl": jnp.concatenate([wr_hi, wr_lo], axis=0), "b": b_router.reshape(N_EXPERTS, 1)}

    for l in range(depth):
        wp = _prep_layer(l, w_in, conv_w, g_q_a, w_uq, g_kv_a, w_ukv, g_qn, g_kn, w_conv_out,
                         w_attn_out, w_o, w_e_gate, w_e_up, w_e_down, w_s_gate, w_s_up, w_s_down)
        sh1, sc1, gt1, sh2, sc2, gt2 = (mod[l, :, j] for j in range(6))
        cb, u, sga, sgb, qt, k, vt = _proj_call(x, g_mix[l].reshape(1, D_MODEL), sc1, sh1, wp, tabs)
        ot = _attn_call(qt, k, vt)
        x = _merge_call(x, cb, u, sga, sgb, ot, gt1, wp)
        x = _moe_call(x, g_ffn[l].reshape(1, D_MODEL), sc2, sh2, gt2, wp, wr)
    return x
```

```python
import functools

import jax
import jax.numpy as jnp
from jax import lax
from jax.experimental import pallas as pl
from jax.experimental.pallas import tpu as pltpu

F32 = jnp.float32
BF16 = jnp.bfloat16

D_MODEL = 1024
CONV_WIDTH = 512
N_HEADS = 8
QK_NOPE = 64
QK_ROPE = 32
V_DIM = 64
QK_DIM = QK_NOPE + QK_ROPE
Q_RANK = 256
KV_RANK = 128
ROPE_THETA = 10000.0
N_EXPERTS = 16
N_GROUPS = 4
GROUP_SIZE = N_EXPERTS // N_GROUPS
D_EXPERT = 256
D_SHARED = 256
EPS = 1e-6

HEAD_PAD = 128
V_ROWS = 80
LAT_START = 3 * CONV_WIDTH
GATE_START = LAT_START + Q_RANK + KV_RANK + 2 * HEAD_PAD
IN_PAD = GATE_START + 2 * D_MODEL
ROPE_HALF = QK_ROPE // 2

TOKEN_TILE = 512
ATTN_TILE = 256
LOG2_E = 1.4426950408889634
VMEM_LIMIT = 56 * 1024 * 1024
NEG_BIG = -1e30


def _const_spec(shape):
    nd = len(shape)
    return pl.BlockSpec(shape, lambda *_: (0,) * nd, pipeline_mode=pl.Buffered(1))


def _silu(v):
    return v * jax.nn.sigmoid(v)


def _mod_kernel(c_ref, w_ref, b_ref, o_ref):
    c = c_ref[...]
    o_ref[...] = jnp.dot(_silu(c), w_ref[...], precision=lax.Precision.HIGHEST,
                         preferred_element_type=F32) + b_ref[...]


def _modulation(c_pad, w_ada, b_ada):
    depth = w_ada.shape[0]
    nblk = w_ada.shape[2] // D_MODEL
    rows = c_pad.shape[0]
    return pl.pallas_call(
        _mod_kernel,
        out_shape=jax.ShapeDtypeStruct((depth, rows, w_ada.shape[2]), F32),
        grid=(depth, nblk),
        in_specs=[
            pl.BlockSpec((rows, D_MODEL), lambda l, j: (0, 0)),
            pl.BlockSpec((None, D_MODEL, D_MODEL), lambda l, j: (l, 0, j)),
            pl.BlockSpec((None, 1, D_MODEL), lambda l, j: (l, 0, j)),
        ],
        out_specs=pl.BlockSpec((None, rows, D_MODEL), lambda l, j: (l, 0, j)),
        compiler_params=pltpu.CompilerParams(
            dimension_semantics=("arbitrary", "arbitrary"), vmem_limit_bytes=VMEM_LIMIT),
        name="adaln_mod",
    )(c_pad, w_ada, b_ada.reshape(depth, 1, -1))


def _store_token_tiles(ref, row0, val):
    rows = val.shape[0]
    for j in range(val.shape[1] // ATTN_TILE):
        ref[j, row0:row0 + rows, :] = val[:, j * ATTN_TILE:(j + 1) * ATTN_TILE]


def _proj_kernel(x_ref, g_ref, sc_ref, sh_ref, win_ref, gqa_ref, wuqt_ref, gkva_ref, wuk_ref,
                 wuvt_ref, gq_ref, gk_ref, gksw_ref, ones_ref, cost_ref, sint_ref, ck_ref, ss_ref,
                 cb_ref, u_ref, sga_ref, sgb_ref, qt_ref, k_ref, vt_ref):
    x = x_ref[...]
    r = lax.rsqrt(jnp.mean(x * x, axis=-1, keepdims=True) + EPS)
    h = (x * r) * (g_ref[...] * (1.0 + sc_ref[...])) + sh_ref[...]
    hb = h.astype(BF16)
    tm = x.shape[0]

    lat = jnp.dot(hb, win_ref[:, LAT_START:GATE_START], preferred_element_type=F32)
    cq = lat[:, 0:Q_RANK]
    ckv = lat[:, Q_RANK:Q_RANK + KV_RANK]
    kpe = lat[:, Q_RANK + KV_RANK:Q_RANK + KV_RANK + HEAD_PAD]
    kpe_sw = lat[:, Q_RANK + KV_RANK + HEAD_PAD:Q_RANK + KV_RANK + 2 * HEAD_PAD]

    cqn = (cq * lax.rsqrt(jnp.mean(cq * cq, axis=-1, keepdims=True) + EPS) * gqa_ref[...]).astype(BF16)
    ckvn = (ckv * lax.rsqrt(jnp.mean(ckv * ckv, axis=-1, keepdims=True) + EPS) * gkva_ref[...]).astype(BF16)

    nt = (((1,), (1,)), ((), ()))
    qt = lax.dot_general(wuqt_ref[...], cqn, nt, preferred_element_type=F32)
    cost = cost_ref[...]
    sint = sint_ref[...]
    gq = gq_ref[...]
    for hh in range(N_HEADS):
        base = hh * HEAD_PAD
        qh = qt[base:base + HEAD_PAD]
        rr = lax.rsqrt(jnp.sum(qh * qh, axis=0, keepdims=True) * (1.0 / QK_DIM) + EPS)
        qn = qh * rr * gq
        x1 = qn[QK_NOPE:QK_NOPE + ROPE_HALF]
        x2 = qn[QK_NOPE + ROPE_HALF:QK_DIM]
        _store_token_tiles(qt_ref, base, qn[0:QK_NOPE].astype(BF16))
        _store_token_tiles(qt_ref, base + QK_NOPE, (x1 * cost - x2 * sint).astype(BF16))
        _store_token_tiles(qt_ref, base + QK_NOPE + ROPE_HALF, (x2 * cost + x1 * sint).astype(BF16))
        _store_token_tiles(qt_ref, base + QK_DIM, jnp.zeros((HEAD_PAD - QK_DIM, tm), BF16))

    kn = jnp.dot(ckvn, wuk_ref[...], preferred_element_type=F32)
    direct = gk_ref[...] * ck_ref[...]
    partner = kpe_sw * (gksw_ref[...] * ss_ref[...])
    ones = ones_ref[...]
    for hh in range(N_HEADS):
        base = hh * HEAD_PAD
        kh = kn[:, base:base + HEAD_PAD] + kpe
        ssq = jnp.dot((kh * kh).astype(BF16), ones, preferred_element_type=F32)
        rr = lax.rsqrt(ssq * (1.0 / QK_DIM) + EPS)
        k_ref[:, base:base + HEAD_PAD] = (rr * (kh * direct + partner)).astype(BF16)

    vt = lax.dot_general(wuvt_ref[...], ckvn, nt, preferred_element_type=F32)
    for hh in range(N_HEADS):
        _store_token_tiles(vt_ref, hh * V_ROWS, vt[hh * V_DIM:(hh + 1) * V_DIM].astype(BF16))
        _store_token_tiles(vt_ref, hh * V_ROWS + V_DIM, jnp.ones((V_ROWS - V_DIM, tm), BF16))

    pc = jnp.dot(hb, win_ref[:, 0:3 * CONV_WIDTH], preferred_element_type=F32)
    cb_ref[...] = pc[:, 0:CONV_WIDTH].astype(BF16)
    u_ref[...] = (pc[:, CONV_WIDTH:2 * CONV_WIDTH] * pc[:, 2 * CONV_WIDTH:3 * CONV_WIDTH]).astype(BF16)

    pg = jnp.dot(hb, win_ref[:, GATE_START:IN_PAD], preferred_element_type=F32)
    sga_ref[...] = jax.nn.sigmoid(pg[:, 0:D_MODEL]).astype(BF16)
    sgb_ref[...] = jax.nn.sigmoid(pg[:, D_MODEL:2 * D_MODEL]).astype(BF16)


def _proj_call(x, g_mix, sc1, sh1, wp, tabs):
    b, s, _ = x.shape
    tm = TOKEN_TILE
    at = ATTN_TILE
    tile = lambda w: pl.BlockSpec((None, tm, w), lambda bi, i: (bi, i, 0))
    tile_t = lambda rws: pl.BlockSpec((None, rws, tm), lambda bi, i: (bi, 0, i))
    per_batch = pl.BlockSpec((None, 1, D_MODEL), lambda bi, i: (bi, 0, 0))
    in_specs = [
        tile(D_MODEL), _const_spec((1, D_MODEL)), per_batch, per_batch,
        _const_spec((D_MODEL, IN_PAD)), _const_spec((1, Q_RANK)),
        _const_spec((N_HEADS * HEAD_PAD, Q_RANK)), _const_spec((1, KV_RANK)),
        _const_spec((KV_RANK, N_HEADS * HEAD_PAD)), _const_spec((N_HEADS * V_DIM, KV_RANK)),
        _const_spec((HEAD_PAD, 1)), _const_spec((1, HEAD_PAD)), _const_spec((1, HEAD_PAD)),
        _const_spec((HEAD_PAD, HEAD_PAD)),
        tile_t(ROPE_HALF), tile_t(ROPE_HALF), tile(HEAD_PAD), tile(HEAD_PAD),
    ]
    out_shape = [
        jax.ShapeDtypeStruct((b, s, CONV_WIDTH), BF16),
        jax.ShapeDtypeStruct((b, s, CONV_WIDTH), BF16),
        jax.ShapeDtypeStruct((b, s, D_MODEL), BF16),
        jax.ShapeDtypeStruct((b, s, D_MODEL), BF16),
        jax.ShapeDtypeStruct((b, s // at, N_HEADS * HEAD_PAD, at), BF16),
        jax.ShapeDtypeStruct((b, s, N_HEADS * HEAD_PAD), BF16),
        jax.ShapeDtypeStruct((b, s // at, N_HEADS * V_ROWS, at), BF16),
    ]
    att_tiles = lambda rws: pl.BlockSpec((None, tm // at, rws, at), lambda bi, i: (bi, i, 0, 0))
    out_specs = [tile(CONV_WIDTH), tile(CONV_WIDTH), tile(D_MODEL), tile(D_MODEL),
                 att_tiles(N_HEADS * HEAD_PAD), tile(N_HEADS * HEAD_PAD), att_tiles(N_HEADS * V_ROWS)]
    return pl.pallas_call(
        _proj_kernel, out_shape=out_shape, grid=(b, s // tm),
        in_specs=in_specs, out_specs=out_specs,
        compiler_params=pltpu.CompilerParams(
            dimension_semantics=("parallel", "parallel"), vmem_limit_bytes=VMEM_LIMIT),
        name="norm_in_proj",
    )(x, g_mix, sc1, sh1, wp["w_in"], wp["g_q_a"], wp["w_uq_t"], wp["g_kv_a"], wp["w_uk"],
      wp["w_uv_t"], wp["g_q"], wp["g_k"], wp["g_k_sw"], jnp.ones((HEAD_PAD, HEAD_PAD), BF16),
      tabs["cos_t"], tabs["sin_t"], tabs["ck"], tabs["ss"])


def _attn_kernel(qt_ref, k_ref, vt_ref, ot_ref, s0, s1, p0, p1):
    s_bufs = (s0, s1)
    p_bufs = (p0, p1)
    nq, _, tq = qt_ref.shape
    nk, _, tk = vt_ref.shape

    def score_chunk(qt, c, dst):
        sc = jnp.dot(k_ref[c * tk:(c + 1) * tk, :], qt, preferred_element_type=F32)
        dst[c] = sc
        return jnp.max(sc, axis=0, keepdims=True)

    def step(i, cur, cmax, alpha_prev, with_scores=True):
        other = 1 - cur
        qt_next = qt_ref[i + 1] if with_scores else None
        m = jnp.full((1, tq), NEG_BIG, F32)
        acc = jnp.zeros((V_ROWS, tq), F32)
        cmax_next, alpha = [], []
        for c in range(nk):
            m_new = jnp.maximum(m, cmax[c])
            alpha.append(jnp.exp2(m - m_new))
            p_bufs[cur][c] = jnp.exp2(s_bufs[cur][c] - m_new).astype(BF16)
            m = m_new
            if with_scores:
                cmax_next.append(score_chunk(qt_next, c, s_bufs[other]))
            acc = alpha_prev[c] * acc + jnp.dot(vt_ref[c], p_bufs[other][c], preferred_element_type=F32)
        ot_ref[jnp.maximum(i - 1, 0)] = (acc[0:V_DIM] / acc[V_DIM:V_DIM + 1]).astype(BF16)
        return cmax_next, alpha

    def trip(t, carry):
        cmax, alpha_prev = step(2 * t, 0, *carry)
        return step(2 * t + 1, 1, cmax, alpha_prev)

    qt0 = qt_ref[0]
    cmax0 = [score_chunk(qt0, c, s0) for c in range(nk)]
    for c in range(nk):
        p1[c] = jnp.ones((tk, tq), BF16)
    ones = [jnp.ones((1, tq), F32)] * nk
    cmax, alpha_prev = lax.fori_loop(0, nq // 2 - 1, trip, (cmax0, ones))
    cmax, alpha_prev = step(nq - 2, 0, cmax, alpha_prev)
    _, alpha_prev = step(nq - 1, 1, cmax, alpha_prev, with_scores=False)
    acc = jnp.zeros((V_ROWS, tq), F32)
    for c in range(nk):
        acc = alpha_prev[c] * acc + jnp.dot(vt_ref[c], p1[c], preferred_element_type=F32)
    ot_ref[nq - 1] = (acc[0:V_DIM] / acc[V_DIM:V_DIM + 1]).astype(BF16)


def _attn_call(qt, k, vt):
    b, nq, _, tq = qt.shape
    _, nk, _, tk = vt.shape
    s = k.shape[1]
    assert nq % 2 == 0 and nq >= 4 and nk * tk == s
    return pl.pallas_call(
        _attn_kernel,
        out_shape=jax.ShapeDtypeStruct((b, nq, N_HEADS * V_DIM, tq), BF16),
        grid=(b, N_HEADS),
        in_specs=[
            pl.BlockSpec((None, nq, HEAD_PAD, tq), lambda bi, hh: (bi, 0, hh, 0)),
            pl.BlockSpec((None, s, HEAD_PAD), lambda bi, hh: (bi, 0, hh)),
            pl.BlockSpec((None, nk, V_ROWS, tk), lambda bi, hh: (bi, 0, hh, 0)),
        ],
        out_specs=pl.BlockSpec((None, nq, V_DIM, tq), lambda bi, hh: (bi, 0, hh, 0)),
        scratch_shapes=[pltpu.VMEM((nk, tk, tq), F32)] * 2 + [pltpu.VMEM((nk, tk, tq), BF16)] * 2,
        compiler_params=pltpu.CompilerParams(
            dimension_semantics=("parallel", "parallel"), vmem_limit_bytes=VMEM_LIMIT),
        name="attention_t",
    )(qt, k, vt)


def _merge_kernel(x_ref, cb_ref, u_ref, uprev_ref, unext_ref, sga_ref, sgb_ref, ot_ref,
                  cw_ref, wco_ref, wao_ref, wo_ref, gt_ref, xo_ref):
    i = pl.program_id(1)
    last = pl.num_programs(1) - 1
    u = u_ref[...].astype(F32)
    tm = u.shape[0]
    halo = uprev_ref.shape[0]
    prev = jnp.where(i > 0, uprev_ref[halo - 1:halo, :].astype(F32), 0.0)
    nxt = jnp.where(i < last, unext_ref[0:1, :].astype(F32), 0.0)
    row = lax.broadcasted_iota(jnp.int32, u.shape, 0)
    u_m = jnp.where(row == 0, prev, pltpu.roll(u, 1, 0))
    u_p = jnp.where(row == tm - 1, nxt, pltpu.roll(u, tm - 1, 0))
    cw = cw_ref[...]
    conv = cw[0:1, :] * u_m + cw[1:2, :] * u + cw[2:3, :] * u_p
    za = (cb_ref[...].astype(F32) * conv).astype(BF16)
    ya = jnp.dot(za, wco_ref[...], preferred_element_type=F32)
    tn = (((0,), (0,)), ((), ()))
    yb = jnp.concatenate(
        [lax.dot_general(ot_ref[j], wao_ref[...], tn, preferred_element_type=F32) for j in range(ot_ref.shape[0])],
        axis=0)
    merged = (sga_ref[...].astype(F32) * ya + sgb_ref[...].astype(F32) * yb).astype(BF16)
    y = jnp.dot(merged, wo_ref[...], preferred_element_type=F32)
    xo_ref[...] = x_ref[...] + gt_ref[...] * y


def _merge_call(x, cb, u, sga, sgb, ot, gt1, wp):
    b, s, _ = x.shape
    tm = TOKEN_TILE
    halo = 16
    nh = tm // halo
    tile = lambda w: pl.BlockSpec((None, tm, w), lambda bi, i: (bi, i, 0))
    per_batch = pl.BlockSpec((None, 1, D_MODEL), lambda bi, i: (bi, 0, 0))
    in_specs = [
        tile(D_MODEL), tile(CONV_WIDTH), tile(CONV_WIDTH),
        pl.BlockSpec((None, halo, CONV_WIDTH), lambda bi, i: (bi, jnp.maximum(i * nh - 1, 0), 0)),
        pl.BlockSpec((None, halo, CONV_WIDTH), lambda bi, i: (bi, jnp.minimum((i + 1) * nh, s // halo - 1), 0)),
        tile(D_MODEL), tile(D_MODEL),
        pl.BlockSpec((None, tm // ATTN_TILE, N_HEADS * V_DIM, ATTN_TILE), lambda bi, i: (bi, i, 0, 0)),
        _const_spec((3, CONV_WIDTH)), _const_spec((CONV_WIDTH, D_MODEL)),
        _const_spec((N_HEADS * V_DIM, D_MODEL)), _const_spec((D_MODEL, D_MODEL)), per_batch,
    ]
    return pl.pallas_call(
        _merge_kernel, out_shape=jax.ShapeDtypeStruct(x.shape, F32), grid=(b, s // tm),
        in_specs=in_specs, out_specs=tile(D_MODEL),
        compiler_params=pltpu.CompilerParams(
            dimension_semantics=("parallel", "parallel"), vmem_limit_bytes=VMEM_LIMIT),
        name="merge_out_proj",
    )(x, cb, u, u, u, sga, sgb, ot, wp["conv_w"], wp["w_conv_out"], wp["w_attn_out"], wp["w_o"], gt1)


def _route(logits_t, bias_col):
    scores = jax.nn.sigmoid(logits_t)
    biased = scores + bias_col
    neg_inf = jnp.float32(-jnp.inf)
    gscores = []
    for g in range(N_GROUPS):
        rws = [biased[g * GROUP_SIZE + j:g * GROUP_SIZE + j + 1] for j in range(GROUP_SIZE)]
        best = None
        for a in range(GROUP_SIZE):
            for bb in range(a + 1, GROUP_SIZE):
                pair = rws[a] + rws[bb]
                best = pair if best is None else jnp.maximum(best, pair)
        gscores.append(best)
    top = gscores[0]
    sel = jnp.zeros_like(top, dtype=jnp.int32)
    for g in range(1, N_GROUPS):
        better = gscores[g] > top
        sel = jnp.where(better, g, sel)
        top = jnp.where(better, gscores[g], top)
    eidx = lax.broadcasted_iota(jnp.int32, biased.shape, 0)
    masked = jnp.where(eidx // GROUP_SIZE == sel, biased, neg_inf)
    m1 = jnp.max(masked, axis=0, keepdims=True)
    i1 = jnp.min(jnp.where(masked == m1, eidx, N_EXPERTS), axis=0, keepdims=True)
    pick1 = eidx == i1
    masked2 = jnp.where(pick1, neg_inf, masked)
    m2 = jnp.max(masked2, axis=0, keepdims=True)
    i2 = jnp.min(jnp.where(masked2 == m2, eidx, N_EXPERTS), axis=0, keepdims=True)
    pick2 = eidx == i2
    s1 = jnp.sum(jnp.where(pick1, scores, 0.0), axis=0, keepdims=True)
    s2 = jnp.sum(jnp.where(pick2, scores, 0.0), axis=0, keepdims=True)
    return jnp.where(pick1 | pick2, scores / (s1 + s2), 0.0)


def _moe_kernel(x_ref, g_ref, sc_ref, sh_ref, gt_ref, wr_ref, br_ref,
                wgu_ref, wdn_ref, wsgu_ref, wsdn_ref, xo_ref, act_ref):
    x = x_ref[...]
    tm = x.shape[0]
    r = lax.rsqrt(jnp.mean(x * x, axis=-1, keepdims=True) + EPS)
    h = (x * r) * (g_ref[...] * (1.0 + sc_ref[...])) + sh_ref[...]
    hi = h.astype(BF16)
    lo = (h - hi.astype(F32)).astype(BF16)

    nt = (((1,), (1,)), ((), ()))
    l_hi = lax.dot_general(wr_ref[...], hi, nt, preferred_element_type=F32)
    l_lo = lax.dot_general(wr_ref[0:N_EXPERTS, :], lo, nt, preferred_element_type=F32)
    logits_t = l_hi[0:N_EXPERTS] + l_hi[N_EXPERTS:2 * N_EXPERTS] + l_lo
    comb_t = _route(logits_t, br_ref[...])
    comb = jnp.concatenate([comb_t, jnp.zeros((128 - N_EXPERTS, tm), F32)], axis=0).T

    for e in range(N_EXPERTS):
        gu = jnp.dot(hi, wgu_ref[e], preferred_element_type=F32)
        a = _silu(gu[:, 0:D_EXPERT]) * gu[:, D_EXPERT:2 * D_EXPERT] * comb[:, e:e + 1]
        act_ref[:, e * D_EXPERT:(e + 1) * D_EXPERT] = a.astype(BF16)
    gs = jnp.dot(hi, wsgu_ref[...], preferred_element_type=F32)
    a_s = _silu(gs[:, 0:D_SHARED]) * gs[:, D_SHARED:2 * D_SHARED]
    y = jnp.dot(act_ref[...], wdn_ref[...], preferred_element_type=F32)
    y = y + jnp.dot(a_s.astype(BF16), wsdn_ref[...], preferred_element_type=F32)
    xo_ref[...] = x + gt_ref[...] * y


def _moe_call(x, g_ffn, sc2, sh2, gt2, wp, wr):
    b, s, _ = x.shape
    tm = TOKEN_TILE
    tile = pl.BlockSpec((None, tm, D_MODEL), lambda bi, i: (bi, i, 0))
    per_batch = pl.BlockSpec((None, 1, D_MODEL), lambda bi, i: (bi, 0, 0))
    in_specs = [
        tile, _const_spec((1, D_MODEL)), per_batch, per_batch, per_batch,
        _const_spec((2 * N_EXPERTS, D_MODEL)), _const_spec((N_EXPERTS, 1)),
        _const_spec((N_EXPERTS, D_MODEL, 2 * D_EXPERT)), _const_spec((N_EXPERTS * D_EXPERT, D_MODEL)),
        _const_spec((D_MODEL, 2 * D_SHARED)), _const_spec((D_SHARED, D_MODEL)),
    ]
    return pl.pallas_call(
        _moe_kernel, out_shape=jax.ShapeDtypeStruct(x.shape, F32), grid=(b, s // tm),
        in_specs=in_specs, out_specs=tile,
        scratch_shapes=[pltpu.VMEM((tm, N_EXPERTS * D_EXPERT), BF16)],
        compiler_params=pltpu.CompilerParams(
            dimension_semantics=("parallel", "parallel"), vmem_limit_bytes=VMEM_LIMIT),
        name="moe_ffn",
    )(x, g_ffn, sc2, sh2, gt2, wr["w_hl"], wr["b"], wp["w_gu"], wp["w_dn"],
      wp["w_s_gu"], wp["w_s_dn"])


def _prep_layer(l, w_in, conv_w, g_q_a, w_uq, g_kv_a, w_ukv, g_qn, g_kn, w_conv_out, w_attn_out,
                w_o, w_e_gate, w_e_up, w_e_down, w_s_gate, w_s_up, w_s_down):
    wi = w_in[l]
    split = 3 * CONV_WIDTH + Q_RANK + KV_RANK
    w_pe = wi[:, split:split + QK_ROPE]
    w_pe_sw = jnp.concatenate([w_pe[:, ROPE_HALF:], w_pe[:, :ROPE_HALF]], axis=1)
    lane_pad = lambda w: jnp.pad(w, ((0, 0), (QK_NOPE, HEAD_PAD - QK_DIM)))
    w_in_r = jnp.concatenate(
        [wi[:, :split], lane_pad(w_pe), lane_pad(w_pe_sw), wi[:, split + QK_ROPE:]], axis=1).astype(BF16)

    wq = w_uq[l].reshape(Q_RANK, N_HEADS, QK_DIM)
    wq = jnp.pad(wq, ((0, 0), (0, 0), (0, HEAD_PAD - QK_DIM))).reshape(Q_RANK, N_HEADS * HEAD_PAD)
    wkv = w_ukv[l].reshape(KV_RANK, N_HEADS, QK_NOPE + V_DIM)
    wk = jnp.pad(wkv[:, :, :QK_NOPE], ((0, 0), (0, 0), (0, HEAD_PAD - QK_NOPE))).reshape(KV_RANK, N_HEADS * HEAD_PAD)
    wv = wkv[:, :, QK_NOPE:].reshape(KV_RANK, N_HEADS * V_DIM)
    scale = QK_DIM ** -0.5 * LOG2_E
    g_q = jnp.pad(g_qn[l] * scale, (0, HEAD_PAD - QK_DIM)).reshape(HEAD_PAD, 1)
    g_k = jnp.pad(g_kn[l], (0, HEAD_PAD - QK_DIM)).reshape(1, HEAD_PAD)
    g_rope = g_kn[l][QK_NOPE:]
    g_k_sw = jnp.pad(jnp.concatenate([g_rope[ROPE_HALF:], g_rope[:ROPE_HALF]]),
                     (QK_NOPE, HEAD_PAD - QK_DIM)).reshape(1, HEAD_PAD)
    return {
        "w_in": w_in_r,
        "g_q_a": g_q_a[l].reshape(1, Q_RANK),
        "w_uq_t": wq.T.astype(BF16),
        "g_kv_a": g_kv_a[l].reshape(1, KV_RANK),
        "w_uk": wk.astype(BF16),
        "w_uv_t": wv.T.astype(BF16),
        "g_q": g_q, "g_k": g_k, "g_k_sw": g_k_sw,
        "conv_w": conv_w[l],
        "w_conv_out": w_conv_out[l].astype(BF16),
        "w_attn_out": w_attn_out[l].astype(BF16),
        "w_o": w_o[l].astype(BF16),
        "w_gu": jnp.concatenate([w_e_gate[l], w_e_up[l]], axis=-1).astype(BF16),
        "w_dn": w_e_down[l].reshape(N_EXPERTS * D_EXPERT, D_MODEL).astype(BF16),
        "w_s_gu": jnp.concatenate([w_s_gate[l], w_s_up[l]], axis=-1).astype(BF16),
        "w_s_dn": w_s_down[l].astype(BF16),
    }


def _rope_tables(positions):
    b, s = positions.shape
    inv_freq = ROPE_THETA ** (-jnp.arange(ROPE_HALF, dtype=F32) / ROPE_HALF)
    ang = positions.astype(F32)[..., None] * inv_freq
    cos = jnp.cos(ang)
    sin = jnp.sin(ang)
    ones64 = jnp.ones((b, s, QK_NOPE), F32)
    zeros32 = jnp.zeros((b, s, HEAD_PAD - QK_DIM), F32)
    zeros64 = jnp.zeros((b, s, QK_NOPE), F32)
    return {
        "cos_t": cos.transpose(0, 2, 1), "sin_t": sin.transpose(0, 2, 1),
        "ck": jnp.concatenate([ones64, cos, cos, zeros32], axis=-1),
        "ss": jnp.concatenate([zeros64, -sin, sin, zeros32], axis=-1),
    }


def kernel(x, c, positions, w_ada, b_ada, g_mix, g_ffn, w_in, conv_w, g_q_a, w_uq, g_kv_a, w_ukv,
           g_qn, g_kn, w_conv_out, w_attn_out, w_o, w_router, b_router, w_e_gate, w_e_up, w_e_down,
           w_s_gate, w_s_up, w_s_down):
    depth = w_ada.shape[0]
    b = x.shape[0]
    c_pad = jnp.pad(c, ((0, 8 - b), (0, 0)))
    mod = _modulation(c_pad, w_ada, b_ada)[:, :b, :]
    mod = mod.reshape(depth, b, 6, 1, D_MODEL)

    tabs = _rope_tables(positions)
    wr_t = w_router.T
    wr_hi = wr_t.astype(BF16)
    wr_lo = (wr_t - wr_hi.astype(F32)).astype(BF16)
    wr = {"w_hl": jnp.concatenate([wr_hi, wr_lo], axis=0), "b": b_router.reshape(N_EXPERTS, 1)}

    for l in range(depth):
        wp = _prep_layer(l, w_in, conv_w, g_q_a, w_uq, g_kv_a, w_ukv, g_qn, g_kn, w_conv_out,
                         w_attn_out, w_o, w_e_gate, w_e_up, w_e_down, w_s_gate, w_s_up, w_s_down)
        sh1, sc1, gt1, sh2, sc2, gt2 = (mod[l, :, j] for j in range(6))
        cb, u, sga, sgb, qt, k, vt = _proj_call(x, g_mix[l].reshape(1, D_MODEL), sc1, sh1, wp, tabs)
        ot = _attn_call(qt, k, vt)
        x = _merge_call(x, cb, u, sga, sgb, ot, gt1, wp)
        x = _moe_call(x, g_ffn[l].reshape(1, D_MODEL), sc2, sh2, gt2, wp, wr)
    return x
```

```python
import functools

import jax
import jax.numpy as jnp
from jax import lax
from jax.experimental import pallas as pl
from jax.experimental.pallas import tpu as pltpu

F32 = jnp.float32
BF16 = jnp.bfloat16

D_MODEL = 1024
CONV_WIDTH = 512
N_HEADS = 8
QK_NOPE = 64
QK_ROPE = 32
V_DIM = 64
QK_DIM = QK_NOPE + QK_ROPE
Q_RANK = 256
KV_RANK = 128
ROPE_THETA = 10000.0
N_EXPERTS = 16
N_GROUPS = 4
GROUP_SIZE = N_EXPERTS // N_GROUPS
D_EXPERT = 256
D_SHARED = 256
EPS = 1e-6

HEAD_PAD = 128
V_ROWS = 80
LAT_START = 3 * CONV_WIDTH
GATE_START = LAT_START + Q_RANK + KV_RANK + 2 * HEAD_PAD
IN_PAD = GATE_START + 2 * D_MODEL
ROPE_HALF = QK_ROPE // 2

TOKEN_TILE = 512
ATTN_TILE = 256
LOG2_E = 1.4426950408889634
VMEM_LIMIT = 56 * 1024 * 1024
NEG_BIG = -1e30


def _const_spec(shape):
    nd = len(shape)
    return pl.BlockSpec(shape, lambda *_: (0,) * nd, pipeline_mode=pl.Buffered(1))


def _layer_spec(shape, layer):
    nd = len(shape)
    return pl.BlockSpec((None,) + tuple(shape), lambda *_: (layer,) + (0,) * nd, pipeline_mode=pl.Buffered(1))


def _silu(v):
    return v * jax.nn.sigmoid(v)


def _split_bf16(v):
    hi = v.astype(BF16)
    return hi, (v - hi.astype(F32)).astype(BF16)


def _mod_kernel(c_ref, w_ref, b_ref, o_ref):
    hi, lo = _split_bf16(_silu(c_ref[...]))
    w = w_ref[...].astype(BF16)
    o_ref[...] = (jnp.dot(hi, w, preferred_element_type=F32) + jnp.dot(lo, w, preferred_element_type=F32)
                  + b_ref[...])


def _modulation(c_pad, w_ada, b_ada):
    depth = w_ada.shape[0]
    nblk = w_ada.shape[2] // D_MODEL
    rows = c_pad.shape[0]
    return pl.pallas_call(
        _mod_kernel,
        out_shape=jax.ShapeDtypeStruct((depth, rows, w_ada.shape[2]), F32),
        grid=(depth, nblk),
        in_specs=[
            pl.BlockSpec((rows, D_MODEL), lambda l, j: (0, 0)),
            pl.BlockSpec((None, D_MODEL, D_MODEL), lambda l, j: (l, 0, j)),
            pl.BlockSpec((None, 1, D_MODEL), lambda l, j: (l, 0, j)),
        ],
        out_specs=pl.BlockSpec((None, rows, D_MODEL), lambda l, j: (l, 0, j)),
        compiler_params=pltpu.CompilerParams(
            dimension_semantics=("arbitrary", "arbitrary"), vmem_limit_bytes=VMEM_LIMIT),
        name="adaln_mod",
    )(c_pad, w_ada, b_ada.reshape(depth, 1, -1))


def _store_token_tiles(ref, row0, val):
    rows = val.shape[0]
    for j in range(val.shape[1] // ATTN_TILE):
        ref[j, row0:row0 + rows, :] = val[:, j * ATTN_TILE:(j + 1) * ATTN_TILE]


def _proj_kernel(x_ref, g_ref, sc_ref, sh_ref, win_ref, gqa_ref, wuqt_ref, gkva_ref, wuk_ref,
                 wuvt_ref, gq_ref, gk_ref, gksw_ref, ones_ref, ecos_ref, esin_ref, cost_ref, sint_ref,
                 cb_ref, u_ref, sga_ref, sgb_ref, qt_ref, k_ref, vt_ref):
    x = x_ref[...]
    r = lax.rsqrt(jnp.mean(x * x, axis=-1, keepdims=True) + EPS)
    h = (x * r) * (g_ref[...] * (1.0 + sc_ref[...])) + sh_ref[...]
    hb = h.astype(BF16)
    tm = x.shape[0]

    lat = jnp.dot(hb, win_ref[:, LAT_START:GATE_START], preferred_element_type=F32)
    cq = lat[:, 0:Q_RANK]
    ckv = lat[:, Q_RANK:Q_RANK + KV_RANK]
    kpe = lat[:, Q_RANK + KV_RANK:Q_RANK + KV_RANK + HEAD_PAD]
    kpe_sw = lat[:, Q_RANK + KV_RANK + HEAD_PAD:Q_RANK + KV_RANK + 2 * HEAD_PAD]

    cqn = (cq * lax.rsqrt(jnp.mean(cq * cq, axis=-1, keepdims=True) + EPS) * gqa_ref[...]).astype(BF16)
    ckvn = (ckv * lax.rsqrt(jnp.mean(ckv * ckv, axis=-1, keepdims=True) + EPS) * gkva_ref[...]).astype(BF16)

    nt = (((1,), (1,)), ((), ()))
    qt = lax.dot_general(wuqt_ref[...], cqn, nt, preferred_element_type=F32)
    cost = cost_ref[...]
    sint = sint_ref[...]
    gq = gq_ref[...]
    for hh in range(N_HEADS):
        base = hh * HEAD_PAD
        qh = qt[base:base + HEAD_PAD]
        rr = lax.rsqrt(jnp.sum(qh * qh, axis=0, keepdims=True) * (1.0 / QK_DIM) + EPS)
        qn = qh * rr * gq
        x1 = qn[QK_NOPE:QK_NOPE + ROPE_HALF]
        x2 = qn[QK_NOPE + ROPE_HALF:QK_DIM]
        _store_token_tiles(qt_ref, base, qn[0:QK_NOPE].astype(BF16))
        _store_token_tiles(qt_ref, base + QK_NOPE, (x1 * cost - x2 * sint).astype(BF16))
        _store_token_tiles(qt_ref, base + QK_NOPE + ROPE_HALF, (x2 * cost + x1 * sint).astype(BF16))
        _store_token_tiles(qt_ref, base + QK_DIM, jnp.zeros((HEAD_PAD - QK_DIM, tm), BF16))

    kn = jnp.dot(ckvn, wuk_ref[...], preferred_element_type=F32)
    tn = (((0,), (0,)), ((), ()))
    place = lambda tab, e: sum(lax.dot_general(part, e, tn, preferred_element_type=F32) for part in _split_bf16(tab))
    lane = lax.broadcasted_iota(jnp.int32, (1, HEAD_PAD), 1)
    ck = jnp.where(lane < QK_NOPE, 1.0, 0.0) + place(cost, ecos_ref[...])
    ss = place(sint, esin_ref[...])
    direct = gk_ref[...] * ck
    partner = kpe_sw * (gksw_ref[...] * ss)
    ones = ones_ref[...]
    for hh in range(N_HEADS):
        base = hh * HEAD_PAD
        kh = kn[:, base:base + HEAD_PAD] + kpe
        ssq = jnp.dot((kh * kh).astype(BF16), ones, preferred_element_type=F32)
        rr = lax.rsqrt(ssq * (1.0 / QK_DIM) + EPS)
        k_ref[:, base:base + HEAD_PAD] = (rr * (kh * direct + partner)).astype(BF16)

    vt = lax.dot_general(wuvt_ref[...], ckvn, nt, preferred_element_type=F32)
    for hh in range(N_HEADS):
        _store_token_tiles(vt_ref, hh * V_ROWS, vt[hh * V_DIM:(hh + 1) * V_DIM].astype(BF16))
        _store_token_tiles(vt_ref, hh * V_ROWS + V_DIM, jnp.ones((V_ROWS - V_DIM, tm), BF16))

    pc = jnp.dot(hb, win_ref[:, 0:3 * CONV_WIDTH], preferred_element_type=F32)
    cb_ref[...] = pc[:, 0:CONV_WIDTH].astype(BF16)
    u_ref[...] = (pc[:, CONV_WIDTH:2 * CONV_WIDTH] * pc[:, 2 * CONV_WIDTH:3 * CONV_WIDTH]).astype(BF16)

    pg = jnp.dot(hb, win_ref[:, GATE_START:IN_PAD], preferred_element_type=F32)
    sga_ref[...] = jax.nn.sigmoid(pg[:, 0:D_MODEL]).astype(BF16)
    sgb_ref[...] = jax.nn.sigmoid(pg[:, D_MODEL:2 * D_MODEL]).astype(BF16)


def _proj_call(layer, x, sc1, sh1, wp, tabs):
    b, s, _ = x.shape
    tm = TOKEN_TILE
    at = ATTN_TILE
    tile = lambda w: pl.BlockSpec((None, tm, w), lambda bi, i: (bi, i, 0))
    tile_t = lambda rws: pl.BlockSpec((None, rws, tm), lambda bi, i: (bi, 0, i))
    per_batch = pl.BlockSpec((None, 1, D_MODEL), lambda bi, i: (bi, 0, 0))
    lw = lambda *shape: _layer_spec(shape, layer)
    in_specs = [
        tile(D_MODEL), lw(1, D_MODEL), per_batch, per_batch,
        lw(D_MODEL, IN_PAD), lw(1, Q_RANK),
        lw(N_HEADS * HEAD_PAD, Q_RANK), lw(1, KV_RANK),
        lw(KV_RANK, N_HEADS * HEAD_PAD), lw(N_HEADS * V_DIM, KV_RANK),
        lw(HEAD_PAD, 1), lw(1, HEAD_PAD), lw(1, HEAD_PAD),
        _const_spec((HEAD_PAD, HEAD_PAD)), _const_spec((ROPE_HALF, HEAD_PAD)), _const_spec((ROPE_HALF, HEAD_PAD)),
        tile_t(ROPE_HALF), tile_t(ROPE_HALF),
    ]
    out_shape = [
        jax.ShapeDtypeStruct((b, s, CONV_WIDTH), BF16),
        jax.ShapeDtypeStruct((b, s, CONV_WIDTH), BF16),
        jax.ShapeDtypeStruct((b, s, D_MODEL), BF16),
        jax.ShapeDtypeStruct((b, s, D_MODEL), BF16),
        jax.ShapeDtypeStruct((b, s // at, N_HEADS * HEAD_PAD, at), BF16),
        jax.ShapeDtypeStruct((b, s, N_HEADS * HEAD_PAD), BF16),
        jax.ShapeDtypeStruct((b, s // at, N_HEADS * V_ROWS, at), BF16),
    ]
    att_tiles = lambda rws: pl.BlockSpec((None, tm // at, rws, at), lambda bi, i: (bi, i, 0, 0))
    out_specs = [tile(CONV_WIDTH), tile(CONV_WIDTH), tile(D_MODEL), tile(D_MODEL),
                 att_tiles(N_HEADS * HEAD_PAD), tile(N_HEADS * HEAD_PAD), att_tiles(N_HEADS * V_ROWS)]
    return pl.pallas_call(
        _proj_kernel, out_shape=out_shape, grid=(b, s // tm),
        in_specs=in_specs, out_specs=out_specs,
        compiler_params=pltpu.CompilerParams(
            dimension_semantics=("parallel", "parallel"), vmem_limit_bytes=VMEM_LIMIT),
        name="norm_in_proj",
    )(x, wp["g_mix"], sc1, sh1, wp["w_in"], wp["g_q_a"], wp["w_uq_t"], wp["g_kv_a"], wp["w_uk"],
      wp["w_uv_t"], wp["g_q"], wp["g_k"], wp["g_k_sw"], jnp.ones((HEAD_PAD, HEAD_PAD), BF16),
      tabs["e_cos"], tabs["e_sin"], tabs["cos_t"], tabs["sin_t"])


def _attn_kernel(qt_ref, k_ref, vt_ref, ot_ref, s0, s1, p0, p1):
    s_bufs = (s0, s1)
    p_bufs = (p0, p1)
    nq, _, tq = qt_ref.shape
    nk, _, tk = vt_ref.shape

    def score_chunk(qt, c, dst):
        sc = jnp.dot(k_ref[c * tk:(c + 1) * tk, :], qt, preferred_element_type=F32)
        dst[c] = sc
        return jnp.max(sc, axis=0, keepdims=True)

    def step(i, cur, cmax, alpha_prev, with_scores=True):
        other = 1 - cur
        qt_next = qt_ref[i + 1] if with_scores else None
        m = jnp.full((1, tq), NEG_BIG, F32)
        acc = jnp.zeros((V_ROWS, tq), F32)
        cmax_next, alpha = [], []
        for c in range(nk):
            m_new = jnp.maximum(m, cmax[c])
            alpha.append(jnp.exp2(m - m_new))
            p_bufs[cur][c] = jnp.exp2(s_bufs[cur][c] - m_new).astype(BF16)
            m = m_new
            if with_scores:
                cmax_next.append(score_chunk(qt_next, c, s_bufs[other]))
            acc = alpha_prev[c] * acc + jnp.dot(vt_ref[c], p_bufs[other][c], preferred_element_type=F32)
        ot_ref[jnp.maximum(i - 1, 0)] = (acc[0:V_DIM] / acc[V_DIM:V_DIM + 1]).astype(BF16)
        return cmax_next, alpha

    def trip(t, carry):
        cmax, alpha_prev = step(2 * t, 0, *carry)
        return step(2 * t + 1, 1, cmax, alpha_prev)

    qt0 = qt_ref[0]
    cmax0 = [score_chunk(qt0, c, s0) for c in range(nk)]
    for c in range(nk):
        p1[c] = jnp.ones((tk, tq), BF16)
    ones = [jnp.ones((1, tq), F32)] * nk
    cmax, alpha_prev = lax.fori_loop(0, nq // 2 - 1, trip, (cmax0, ones))
    cmax, alpha_prev = step(nq - 2, 0, cmax, alpha_prev)
    _, alpha_prev = step(nq - 1, 1, cmax, alpha_prev, with_scores=False)
    acc = jnp.zeros((V_ROWS, tq), F32)
    for c in range(nk):
        acc = alpha_prev[c] * acc + jnp.dot(vt_ref[c], p1[c], preferred_element_type=F32)
    ot_ref[nq - 1] = (acc[0:V_DIM] / acc[V_DIM:V_DIM + 1]).astype(BF16)


def _attn_call(qt, k, vt):
    b, nq, _, tq = qt.shape
    _, nk, _, tk = vt.shape
    s = k.shape[1]
    assert nq % 2 == 0 and nq >= 4 and nk * tk == s
    return pl.pallas_call(
        _attn_kernel,
        out_shape=jax.ShapeDtypeStruct((b, nq, N_HEADS * V_DIM, tq), BF16),
        grid=(b, N_HEADS),
        in_specs=[
            pl.BlockSpec((None, nq, HEAD_PAD, tq), lambda bi, hh: (bi, 0, hh, 0)),
            pl.BlockSpec((None, s, HEAD_PAD), lambda bi, hh: (bi, 0, hh)),
            pl.BlockSpec((None, nk, V_ROWS, tk), lambda bi, hh: (bi, 0, hh, 0)),
        ],
        out_specs=pl.BlockSpec((None, nq, V_DIM, tq), lambda bi, hh: (bi, 0, hh, 0)),
        scratch_shapes=[pltpu.VMEM((nk, tk, tq), F32)] * 2 + [pltpu.VMEM((nk, tk, tq), BF16)] * 2,
        compiler_params=pltpu.CompilerParams(
            dimension_semantics=("parallel", "parallel"), vmem_limit_bytes=VMEM_LIMIT),
        name="attention_t",
    )(qt, k, vt)


def _merge_kernel(x_ref, cb_ref, u_ref, uprev_ref, unext_ref, sga_ref, sgb_ref, ot_ref,
                  cw_ref, wco_ref, wao_ref, wo_ref, gt_ref, xo_ref):
    i = pl.program_id(1)
    last = pl.num_programs(1) - 1
    u = u_ref[...].astype(F32)
    tm = u.shape[0]
    halo = uprev_ref.shape[0]
    prev = jnp.where(i > 0, uprev_ref[halo - 1:halo, :].astype(F32), 0.0)
    nxt = jnp.where(i < last, unext_ref[0:1, :].astype(F32), 0.0)
    row = lax.broadcasted_iota(jnp.int32, u.shape, 0)
    u_m = jnp.where(row == 0, prev, pltpu.roll(u, 1, 0))
    u_p = jnp.where(row == tm - 1, nxt, pltpu.roll(u, tm - 1, 0))
    cw = cw_ref[...]
    conv = cw[0:1, :] * u_m + cw[1:2, :] * u + cw[2:3, :] * u_p
    za = (cb_ref[...].astype(F32) * conv).astype(BF16)
    ya = jnp.dot(za, wco_ref[...], preferred_element_type=F32)
    tn = (((0,), (0,)), ((), ()))
    yb = jnp.concatenate(
        [lax.dot_general(ot_ref[j], wao_ref[...], tn, preferred_element_type=F32) for j in range(ot_ref.shape[0])],
        axis=0)
    merged = (sga_ref[...].astype(F32) * ya + sgb_ref[...].astype(F32) * yb).astype(BF16)
    y = jnp.dot(merged, wo_ref[...], preferred_element_type=F32)
    xo_ref[...] = x_ref[...] + gt_ref[...] * y


def _merge_call(layer, x, cb, u, sga, sgb, ot, gt1, wp):
    b, s, _ = x.shape
    lw = lambda *shape: _layer_spec(shape, layer)
    tm = TOKEN_TILE
    halo = 16
    nh = tm // halo
    tile = lambda w: pl.BlockSpec((None, tm, w), lambda bi, i: (bi, i, 0))
    per_batch = pl.BlockSpec((None, 1, D_MODEL), lambda bi, i: (bi, 0, 0))
    in_specs = [
        tile(D_MODEL), tile(CONV_WIDTH), tile(CONV_WIDTH),
        pl.BlockSpec((None, halo, CONV_WIDTH), lambda bi, i: (bi, jnp.maximum(i * nh - 1, 0), 0)),
        pl.BlockSpec((None, halo, CONV_WIDTH), lambda bi, i: (bi, jnp.minimum((i + 1) * nh, s // halo - 1), 0)),
        tile(D_MODEL), tile(D_MODEL),
        pl.BlockSpec((None, tm // ATTN_TILE, N_HEADS * V_DIM, ATTN_TILE), lambda bi, i: (bi, i, 0, 0)),
        lw(3, CONV_WIDTH), lw(CONV_WIDTH, D_MODEL), lw(N_HEADS * V_DIM, D_MODEL), lw(D_MODEL, D_MODEL), per_batch,
    ]
    return pl.pallas_call(
        _merge_kernel, out_shape=jax.ShapeDtypeStruct(x.shape, F32), grid=(b, s // tm),
        in_specs=in_specs, out_specs=tile(D_MODEL),
        compiler_params=pltpu.CompilerParams(
            dimension_semantics=("parallel", "parallel"), vmem_limit_bytes=VMEM_LIMIT),
        name="merge_out_proj",
    )(x, cb, u, u, u, sga, sgb, ot, wp["conv_w"], wp["w_conv_out"], wp["w_attn_out"], wp["w_o"], gt1)


def _route(logits_t, bias_col):
    scores = jax.nn.sigmoid(logits_t)
    biased = scores + bias_col
    neg_inf = jnp.float32(-jnp.inf)
    gscores = []
    for g in range(N_GROUPS):
        rws = [biased[g * GROUP_SIZE + j:g * GROUP_SIZE + j + 1] for j in range(GROUP_SIZE)]
        best = None
        for a in range(GROUP_SIZE):
            for bb in range(a + 1, GROUP_SIZE):
                pair = rws[a] + rws[bb]
                best = pair if best is None else jnp.maximum(best, pair)
        gscores.append(best)
    top = gscores[0]
    sel = jnp.zeros_like(top, dtype=jnp.int32)
    for g in range(1, N_GROUPS):
        better = gscores[g] > top
        sel = jnp.where(better, g, sel)
        top = jnp.where(better, gscores[g], top)
    eidx = lax.broadcasted_iota(jnp.int32, biased.shape, 0)
    masked = jnp.where(eidx // GROUP_SIZE == sel, biased, neg_inf)
    m1 = jnp.max(masked, axis=0, keepdims=True)
    i1 = jnp.min(jnp.where(masked == m1, eidx, N_EXPERTS), axis=0, keepdims=True)
    pick1 = eidx == i1
    masked2 = jnp.where(pick1, neg_inf, masked)
    m2 = jnp.max(masked2, axis=0, keepdims=True)
    i2 = jnp.min(jnp.where(masked2 == m2, eidx, N_EXPERTS), axis=0, keepdims=True)
    pick2 = eidx == i2
    s1 = jnp.sum(jnp.where(pick1, scores, 0.0), axis=0, keepdims=True)
    s2 = jnp.sum(jnp.where(pick2, scores, 0.0), axis=0, keepdims=True)
    return jnp.where(pick1 | pick2, scores / (s1 + s2), 0.0)


def _moe_kernel(x_ref, g_ref, sc_ref, sh_ref, gt_ref, wr_ref, br_ref,
                wg_ref, wu_ref, wdn_ref, wsg_ref, wsu_ref, wsdn_ref, xo_ref, act_ref):
    x = x_ref[...]
    tm = x.shape[0]
    r = lax.rsqrt(jnp.mean(x * x, axis=-1, keepdims=True) + EPS)
    h = (x * r) * (g_ref[...] * (1.0 + sc_ref[...])) + sh_ref[...]
    hi, lo = _split_bf16(h)

    nt = (((1,), (1,)), ((), ()))
    l_hi = lax.dot_general(wr_ref[...], hi, nt, preferred_element_type=F32)
    l_lo = lax.dot_general(wr_ref[0:N_EXPERTS, :], lo, nt, preferred_element_type=F32)
    logits_t = l_hi[0:N_EXPERTS] + l_hi[N_EXPERTS:2 * N_EXPERTS] + l_lo
    comb_t = _route(logits_t, br_ref[...])
    comb = jnp.concatenate([comb_t, jnp.zeros((128 - N_EXPERTS, tm), F32)], axis=0).T

    for e in range(N_EXPERTS):
        gate = jnp.dot(hi, wg_ref[e], preferred_element_type=F32)
        up = jnp.dot(hi, wu_ref[e], preferred_element_type=F32)
        act_ref[:, e * D_EXPERT:(e + 1) * D_EXPERT] = (_silu(gate) * up * comb[:, e:e + 1]).astype(BF16)
    a_s = _silu(jnp.dot(hi, wsg_ref[...], preferred_element_type=F32)) * jnp.dot(
        hi, wsu_ref[...], preferred_element_type=F32)
    y = jnp.dot(act_ref[...], wdn_ref[...], preferred_element_type=F32)
    y = y + jnp.dot(a_s.astype(BF16), wsdn_ref[...], preferred_element_type=F32)
    xo_ref[...] = x + gt_ref[...] * y


def _moe_call(layer, x, sc2, sh2, gt2, wp, wr):
    b, s, _ = x.shape
    tm = TOKEN_TILE
    tile = pl.BlockSpec((None, tm, D_MODEL), lambda bi, i: (bi, i, 0))
    per_batch = pl.BlockSpec((None, 1, D_MODEL), lambda bi, i: (bi, 0, 0))
    lw = lambda *shape: _layer_spec(shape, layer)
    in_specs = [
        tile, lw(1, D_MODEL), per_batch, per_batch, per_batch,
        _const_spec((2 * N_EXPERTS, D_MODEL)), _const_spec((N_EXPERTS, 1)),
        lw(N_EXPERTS, D_MODEL, D_EXPERT), lw(N_EXPERTS, D_MODEL, D_EXPERT), lw(N_EXPERTS * D_EXPERT, D_MODEL),
        lw(D_MODEL, D_SHARED), lw(D_MODEL, D_SHARED), lw(D_SHARED, D_MODEL),
    ]
    return pl.pallas_call(
        _moe_kernel, out_shape=jax.ShapeDtypeStruct(x.shape, F32), grid=(b, s // tm),
        in_specs=in_specs, out_specs=tile,
        scratch_shapes=[pltpu.VMEM((tm, N_EXPERTS * D_EXPERT), BF16)],
        compiler_params=pltpu.CompilerParams(
            dimension_semantics=("parallel", "parallel"), vmem_limit_bytes=VMEM_LIMIT),
        name="moe_ffn",
    )(x, wp["g_ffn"], sc2, sh2, gt2, wr["w_hl"], wr["b"], wp["w_e_gate"], wp["w_e_up"], wp["w_e_down"],
      wp["w_s_gate"], wp["w_s_up"], wp["w_s_down"])


def _prep_weights(g_mix, g_ffn, w_in, conv_w, g_q_a, w_uq, g_kv_a, w_ukv, g_qn, g_kn, w_conv_out, w_attn_out,
                  w_o, w_e_gate, w_e_up, w_e_down, w_s_gate, w_s_up, w_s_down):
    depth = w_in.shape[0]
    split = 3 * CONV_WIDTH + Q_RANK + KV_RANK
    w_pe = w_in[:, :, split:split + QK_ROPE]
    w_pe_sw = jnp.concatenate([w_pe[:, :, ROPE_HALF:], w_pe[:, :, :ROPE_HALF]], axis=2)
    lane_pad = lambda w: jnp.pad(w, ((0, 0), (0, 0), (QK_NOPE, HEAD_PAD - QK_DIM)))
    w_in_r = jnp.concatenate(
        [w_in[:, :, :split], lane_pad(w_pe), lane_pad(w_pe_sw), w_in[:, :, split + QK_ROPE:]], axis=2).astype(BF16)

    wq = w_uq.reshape(depth, Q_RANK, N_HEADS, QK_DIM)
    wq = jnp.pad(wq, ((0, 0), (0, 0), (0, 0), (0, HEAD_PAD - QK_DIM))).reshape(depth, Q_RANK, N_HEADS * HEAD_PAD)
    wkv = w_ukv.reshape(depth, KV_RANK, N_HEADS, QK_NOPE + V_DIM)
    wk = jnp.pad(wkv[..., :QK_NOPE], ((0, 0), (0, 0), (0, 0), (0, HEAD_PAD - QK_NOPE)))
    wk = wk.reshape(depth, KV_RANK, N_HEADS * HEAD_PAD)
    wv = wkv[..., QK_NOPE:].reshape(depth, KV_RANK, N_HEADS * V_DIM)
    scale = QK_DIM ** -0.5 * LOG2_E
    g_q = jnp.pad(g_qn * scale, ((0, 0), (0, HEAD_PAD - QK_DIM))).reshape(depth, HEAD_PAD, 1)
    g_k = jnp.pad(g_kn, ((0, 0), (0, HEAD_PAD - QK_DIM))).reshape(depth, 1, HEAD_PAD)
    g_rope = g_kn[:, QK_NOPE:]
    g_k_sw = jnp.pad(jnp.concatenate([g_rope[:, ROPE_HALF:], g_rope[:, :ROPE_HALF]], axis=1),
                     ((0, 0), (QK_NOPE, HEAD_PAD - QK_DIM))).reshape(depth, 1, HEAD_PAD)
    return {
        "g_mix": g_mix.reshape(depth, 1, D_MODEL),
        "g_ffn": g_ffn.reshape(depth, 1, D_MODEL),
        "w_in": w_in_r,
        "g_q_a": g_q_a.reshape(depth, 1, Q_RANK),
        "w_uq_t": wq.transpose(0, 2, 1).astype(BF16),
        "g_kv_a": g_kv_a.reshape(depth, 1, KV_RANK),
        "w_uk": wk.astype(BF16),
        "w_uv_t": wv.transpose(0, 2, 1).astype(BF16),
        "g_q": g_q, "g_k": g_k, "g_k_sw": g_k_sw,
        "conv_w": conv_w,
        "w_conv_out": w_conv_out.astype(BF16),
        "w_attn_out": w_attn_out.astype(BF16),
        "w_o": w_o.astype(BF16),
        "w_e_gate": w_e_gate.astype(BF16),
        "w_e_up": w_e_up.astype(BF16),
        "w_e_down": w_e_down.astype(BF16).reshape(depth, N_EXPERTS * D_EXPERT, D_MODEL),
        "w_s_gate": w_s_gate.astype(BF16),
        "w_s_up": w_s_up.astype(BF16),
        "w_s_down": w_s_down.astype(BF16),
    }


def _rope_tables(positions):
    inv_freq = ROPE_THETA ** (-jnp.arange(ROPE_HALF, dtype=F32) / ROPE_HALF)
    ang_t = inv_freq[None, :, None] * positions.astype(F32)[:, None, :]
    row = jnp.arange(ROPE_HALF)[:, None]
    lane = jnp.arange(HEAD_PAD)[None, :]
    first = lane == row + QK_NOPE
    second = lane == row + QK_NOPE + ROPE_HALF
    return {
        "cos_t": jnp.cos(ang_t), "sin_t": jnp.sin(ang_t),
        "e_cos": (first | second).astype(BF16),
        "e_sin": (second.astype(F32) - first.astype(F32)).astype(BF16),
    }


def kernel(x, c, positions, w_ada, b_ada, g_mix, g_ffn, w_in, conv_w, g_q_a, w_uq, g_kv_a, w_ukv,
           g_qn, g_kn, w_conv_out, w_attn_out, w_o, w_router, b_router, w_e_gate, w_e_up, w_e_down,
           w_s_gate, w_s_up, w_s_down):
    depth = w_ada.shape[0]
    b = x.shape[0]
    c_pad = jnp.pad(c, ((0, 8 - b), (0, 0)))
    mod = _modulation(c_pad, w_ada, b_ada)[:, :b, :]
    mod = mod.reshape(depth, b, 6, 1, D_MODEL)

    tabs = _rope_tables(positions)
    wr_hi, wr_lo = _split_bf16(w_router.T)
    wr = {"w_hl": jnp.concatenate([wr_hi, wr_lo], axis=0), "b": b_router.reshape(N_EXPERTS, 1)}
    wp = _prep_weights(g_mix, g_ffn, w_in, conv_w, g_q_a, w_uq, g_kv_a, w_ukv, g_qn, g_kn, w_conv_out,
                       w_attn_out, w_o, w_e_gate, w_e_up, w_e_down, w_s_gate, w_s_up, w_s_down)

    for l in range(depth):
        sh1, sc1, gt1, sh2, sc2, gt2 = (mod[l, :, j] for j in range(6))
        cb, u, sga, sgb, qt, k, vt = _proj_call(l, x, sc1, sh1, wp, tabs)
        ot = _attn_call(qt, k, vt)
        x = _merge_call(l, x, cb, u, sga, sgb, ot, gt1, wp)
        x = _moe_call(l, x, sc2, sh2, gt2, wp, wr)
    return x
```

```python
import functools

import jax
import jax.numpy as jnp
from jax import lax
from jax.experimental import pallas as pl
from jax.experimental.pallas import tpu as pltpu

F32 = jnp.float32
BF16 = jnp.bfloat16

D_MODEL = 1024
CONV_WIDTH = 512
N_HEADS = 8
QK_NOPE = 64
QK_ROPE = 32
V_DIM = 64
QK_DIM = QK_NOPE + QK_ROPE
Q_RANK = 256
KV_RANK = 128
ROPE_THETA = 10000.0
N_EXPERTS = 16
N_GROUPS = 4
GROUP_SIZE = N_EXPERTS // N_GROUPS
D_EXPERT = 256
D_SHARED = 256
EPS = 1e-6

HEAD_PAD = 128
V_ROWS = 80
LAT_START = 3 * CONV_WIDTH
GATE_START = LAT_START + Q_RANK + KV_RANK + 2 * HEAD_PAD
IN_PAD = GATE_START + 2 * D_MODEL
ROPE_HALF = QK_ROPE // 2

TOKEN_TILE = 512
ATTN_TILE = 256
ROW_BLOCK = 128
LOG2_E = 1.4426950408889634
VMEM_LIMIT = 56 * 1024 * 1024
NEG_BIG = -1e30


def _const_spec(shape):
    nd = len(shape)
    return pl.BlockSpec(shape, lambda *_: (0,) * nd, pipeline_mode=pl.Buffered(1))


def _layer_spec(shape, layer):
    nd = len(shape)
    return pl.BlockSpec((None,) + tuple(shape), lambda *_: (layer,) + (0,) * nd, pipeline_mode=pl.Buffered(1))


def _silu(v):
    return v * jax.nn.sigmoid(v)


def _split_bf16(v):
    hi = v.astype(BF16)
    return hi, (v - hi.astype(F32)).astype(BF16)


def _mod_kernel(c_ref, w_ref, b_ref, o_ref):
    hi, lo = _split_bf16(_silu(c_ref[...]))
    w = w_ref[...].astype(BF16)
    o_ref[...] = (jnp.dot(hi, w, preferred_element_type=F32) + jnp.dot(lo, w, preferred_element_type=F32)
                  + b_ref[...])


def _modulation(c_pad, w_ada, b_ada):
    depth = w_ada.shape[0]
    nblk = w_ada.shape[2] // D_MODEL
    rows = c_pad.shape[0]
    return pl.pallas_call(
        _mod_kernel,
        out_shape=jax.ShapeDtypeStruct((depth, rows, w_ada.shape[2]), F32),
        grid=(depth, nblk),
        in_specs=[
            pl.BlockSpec((rows, D_MODEL), lambda l, j: (0, 0)),
            pl.BlockSpec((None, D_MODEL, D_MODEL), lambda l, j: (l, 0, j)),
            pl.BlockSpec((None, 1, D_MODEL), lambda l, j: (l, 0, j)),
        ],
        out_specs=pl.BlockSpec((None, rows, D_MODEL), lambda l, j: (l, 0, j)),
        compiler_params=pltpu.CompilerParams(
            dimension_semantics=("arbitrary", "arbitrary"), vmem_limit_bytes=VMEM_LIMIT),
        name="adaln_mod",
    )(c_pad, w_ada, b_ada.reshape(depth, 1, -1))


def _store_token_tiles(ref, row0, val):
    rows = val.shape[0]
    for j in range(val.shape[1] // ATTN_TILE):
        ref[j, row0:row0 + rows, :] = val[:, j * ATTN_TILE:(j + 1) * ATTN_TILE]


def _proj_kernel(x_ref, g_ref, sc_ref, sh_ref, win_ref, gqa_ref, wuqt_ref, gkva_ref, wuk_ref,
                 wuvt_ref, gq_ref, gk_ref, gksw_ref, ones_ref, ecos_ref, esin_ref, cost_ref, sint_ref,
                 cb_ref, u_ref, sga_ref, sgb_ref, qt_ref, k_ref, vt_ref):
    x = x_ref[...]
    r = lax.rsqrt(jnp.mean(x * x, axis=-1, keepdims=True) + EPS)
    h = (x * r) * (g_ref[...] * (1.0 + sc_ref[...])) + sh_ref[...]
    hb = h.astype(BF16)
    tm = x.shape[0]

    lat = jnp.dot(hb, win_ref[:, LAT_START:GATE_START], preferred_element_type=F32)
    cq = lat[:, 0:Q_RANK]
    ckv = lat[:, Q_RANK:Q_RANK + KV_RANK]
    kpe = lat[:, Q_RANK + KV_RANK:Q_RANK + KV_RANK + HEAD_PAD]
    kpe_sw = lat[:, Q_RANK + KV_RANK + HEAD_PAD:Q_RANK + KV_RANK + 2 * HEAD_PAD]

    cqn = (cq * lax.rsqrt(jnp.mean(cq * cq, axis=-1, keepdims=True) + EPS) * gqa_ref[...]).astype(BF16)
    ckvn = (ckv * lax.rsqrt(jnp.mean(ckv * ckv, axis=-1, keepdims=True) + EPS) * gkva_ref[...]).astype(BF16)

    nt = (((1,), (1,)), ((), ()))
    qt = lax.dot_general(wuqt_ref[...], cqn, nt, preferred_element_type=F32)
    cost = cost_ref[...]
    sint = sint_ref[...]
    gq = gq_ref[...]
    for hh in range(N_HEADS):
        base = hh * HEAD_PAD
        qh = qt[base:base + HEAD_PAD]
        rr = lax.rsqrt(jnp.sum(qh * qh, axis=0, keepdims=True) * (1.0 / QK_DIM) + EPS)
        qn = qh * rr * gq
        x1 = qn[QK_NOPE:QK_NOPE + ROPE_HALF]
        x2 = qn[QK_NOPE + ROPE_HALF:QK_DIM]
        _store_token_tiles(qt_ref, base, qn[0:QK_NOPE].astype(BF16))
        _store_token_tiles(qt_ref, base + QK_NOPE, (x1 * cost - x2 * sint).astype(BF16))
        _store_token_tiles(qt_ref, base + QK_NOPE + ROPE_HALF, (x2 * cost + x1 * sint).astype(BF16))
        _store_token_tiles(qt_ref, base + QK_DIM, jnp.zeros((HEAD_PAD - QK_DIM, tm), BF16))

    kn = jnp.dot(ckvn, wuk_ref[...], preferred_element_type=F32)
    tn = (((0,), (0,)), ((), ()))
    place = lambda tab, e: sum(lax.dot_general(part, e, tn, preferred_element_type=F32) for part in _split_bf16(tab))
    lane = lax.broadcasted_iota(jnp.int32, (1, HEAD_PAD), 1)
    ck = jnp.where(lane < QK_NOPE, 1.0, 0.0) + place(cost, ecos_ref[...])
    ss = place(sint, esin_ref[...])
    direct = gk_ref[...] * ck
    partner = kpe_sw * (gksw_ref[...] * ss)
    ones = ones_ref[...]
    for hh in range(N_HEADS):
        base = hh * HEAD_PAD
        kh = kn[:, base:base + HEAD_PAD] + kpe
        ssq = jnp.dot((kh * kh).astype(BF16), ones, preferred_element_type=F32)
        rr = lax.rsqrt(ssq * (1.0 / QK_DIM) + EPS)
        k_ref[:, base:base + HEAD_PAD] = (rr * (kh * direct + partner)).astype(BF16)

    vt = lax.dot_general(wuvt_ref[...], ckvn, nt, preferred_element_type=F32)
    for hh in range(N_HEADS):
        _store_token_tiles(vt_ref, hh * V_ROWS, vt[hh * V_DIM:(hh + 1) * V_DIM].astype(BF16))
        _store_token_tiles(vt_ref, hh * V_ROWS + V_DIM, jnp.ones((V_ROWS - V_DIM, tm), BF16))

    pc = jnp.dot(hb, win_ref[:, 0:3 * CONV_WIDTH], preferred_element_type=F32)
    cb_ref[...] = pc[:, 0:CONV_WIDTH].astype(BF16)
    u_ref[...] = (pc[:, CONV_WIDTH:2 * CONV_WIDTH] * pc[:, 2 * CONV_WIDTH:3 * CONV_WIDTH]).astype(BF16)

    pg = jnp.dot(hb, win_ref[:, GATE_START:IN_PAD], preferred_element_type=F32)
    sga_ref[...] = jax.nn.sigmoid(pg[:, 0:D_MODEL]).astype(BF16)
    sgb_ref[...] = jax.nn.sigmoid(pg[:, D_MODEL:2 * D_MODEL]).astype(BF16)


def _proj_call(layer, x, sc1, sh1, wp, tabs):
    b, s, _ = x.shape
    tm = TOKEN_TILE
    at = ATTN_TILE
    tile = lambda w: pl.BlockSpec((None, tm, w), lambda bi, i: (bi, i, 0))
    tile_t = lambda rws: pl.BlockSpec((None, rws, tm), lambda bi, i: (bi, 0, i))
    per_batch = pl.BlockSpec((None, 1, D_MODEL), lambda bi, i: (bi, 0, 0))
    lw = lambda *shape: _layer_spec(shape, layer)
    in_specs = [
        tile(D_MODEL), lw(1, D_MODEL), per_batch, per_batch,
        lw(D_MODEL, IN_PAD), lw(1, Q_RANK),
        lw(N_HEADS * HEAD_PAD, Q_RANK), lw(1, KV_RANK),
        lw(KV_RANK, N_HEADS * HEAD_PAD), lw(N_HEADS * V_DIM, KV_RANK),
        lw(HEAD_PAD, 1), lw(1, HEAD_PAD), lw(1, HEAD_PAD),
        _const_spec((HEAD_PAD, HEAD_PAD)), _const_spec((ROPE_HALF, HEAD_PAD)), _const_spec((ROPE_HALF, HEAD_PAD)),
        tile_t(ROPE_HALF), tile_t(ROPE_HALF),
    ]
    out_shape = [
        jax.ShapeDtypeStruct((b, s, CONV_WIDTH), BF16),
        jax.ShapeDtypeStruct((b, s, CONV_WIDTH), BF16),
        jax.ShapeDtypeStruct((b, s, D_MODEL), BF16),
        jax.ShapeDtypeStruct((b, s, D_MODEL), BF16),
        jax.ShapeDtypeStruct((b, s // at, N_HEADS * HEAD_PAD, at), BF16),
        jax.ShapeDtypeStruct((b, s, N_HEADS * HEAD_PAD), BF16),
        jax.ShapeDtypeStruct((b, s // at, N_HEADS * V_ROWS, at), BF16),
    ]
    att_tiles = lambda rws: pl.BlockSpec((None, tm // at, rws, at), lambda bi, i: (bi, i, 0, 0))
    out_specs = [tile(CONV_WIDTH), tile(CONV_WIDTH), tile(D_MODEL), tile(D_MODEL),
                 att_tiles(N_HEADS * HEAD_PAD), tile(N_HEADS * HEAD_PAD), att_tiles(N_HEADS * V_ROWS)]
    return pl.pallas_call(
        _proj_kernel, out_shape=out_shape, grid=(b, s // tm),
        in_specs=in_specs, out_specs=out_specs,
        compiler_params=pltpu.CompilerParams(
            dimension_semantics=("parallel", "parallel"), vmem_limit_bytes=VMEM_LIMIT),
        name="norm_in_proj",
    )(x, wp["g_mix"], sc1, sh1, wp["w_in"], wp["g_q_a"], wp["w_uq_t"], wp["g_kv_a"], wp["w_uk"],
      wp["w_uv_t"], wp["g_q"], wp["g_k"], wp["g_k_sw"], jnp.ones((HEAD_PAD, HEAD_PAD), BF16),
      tabs["e_cos"], tabs["e_sin"], tabs["cos_t"], tabs["sin_t"])


def _attn_kernel(qt_ref, k_ref, vt_ref, ot_ref, s0, s1, p0, p1):
    s_bufs = (s0, s1)
    p_bufs = (p0, p1)
    nq, _, tq = qt_ref.shape
    nk, _, tk = vt_ref.shape

    def score_chunk(qt, c, dst):
        sc = jnp.dot(k_ref[c * tk:(c + 1) * tk, :], qt, preferred_element_type=F32)
        dst[c] = sc
        return jnp.max(sc, axis=0, keepdims=True)

    def step(i, cur, cmax, alpha_prev, with_scores=True):
        other = 1 - cur
        qt_next = qt_ref[i + 1] if with_scores else None
        m = jnp.full((1, tq), NEG_BIG, F32)
        acc = jnp.zeros((V_ROWS, tq), F32)
        cmax_next, alpha = [], []
        for c in range(nk):
            m_new = jnp.maximum(m, cmax[c])
            alpha.append(jnp.exp2(m - m_new))
            p_bufs[cur][c] = jnp.exp2(s_bufs[cur][c] - m_new).astype(BF16)
            m = m_new
            if with_scores:
                cmax_next.append(score_chunk(qt_next, c, s_bufs[other]))
            acc = alpha_prev[c] * acc + jnp.dot(vt_ref[c], p_bufs[other][c], preferred_element_type=F32)
        ot_ref[jnp.maximum(i - 1, 0)] = (acc[0:V_DIM] / acc[V_DIM:V_DIM + 1]).astype(BF16)
        return cmax_next, alpha

    def trip(t, carry):
        cmax, alpha_prev = step(2 * t, 0, *carry)
        return step(2 * t + 1, 1, cmax, alpha_prev)

    qt0 = qt_ref[0]
    cmax0 = [score_chunk(qt0, c, s0) for c in range(nk)]
    for c in range(nk):
        p1[c] = jnp.ones((tk, tq), BF16)
    ones = [jnp.ones((1, tq), F32)] * nk
    cmax, alpha_prev = lax.fori_loop(0, nq // 2 - 1, trip, (cmax0, ones))
    cmax, alpha_prev = step(nq - 2, 0, cmax, alpha_prev)
    _, alpha_prev = step(nq - 1, 1, cmax, alpha_prev, with_scores=False)
    acc = jnp.zeros((V_ROWS, tq), F32)
    for c in range(nk):
        acc = alpha_prev[c] * acc + jnp.dot(vt_ref[c], p1[c], preferred_element_type=F32)
    ot_ref[nq - 1] = (acc[0:V_DIM] / acc[V_DIM:V_DIM + 1]).astype(BF16)


def _attn_call(qt, k, vt):
    b, nq, _, tq = qt.shape
    _, nk, _, tk = vt.shape
    s = k.shape[1]
    assert nq % 2 == 0 and nq >= 4 and nk * tk == s
    return pl.pallas_call(
        _attn_kernel,
        out_shape=jax.ShapeDtypeStruct((b, nq, N_HEADS * V_DIM, tq), BF16),
        grid=(b, N_HEADS),
        in_specs=[
            pl.BlockSpec((None, nq, HEAD_PAD, tq), lambda bi, hh: (bi, 0, hh, 0)),
            pl.BlockSpec((None, s, HEAD_PAD), lambda bi, hh: (bi, 0, hh)),
            pl.BlockSpec((None, nk, V_ROWS, tk), lambda bi, hh: (bi, 0, hh, 0)),
        ],
        out_specs=pl.BlockSpec((None, nq, V_DIM, tq), lambda bi, hh: (bi, 0, hh, 0)),
        scratch_shapes=[pltpu.VMEM((nk, tk, tq), F32)] * 2 + [pltpu.VMEM((nk, tk, tq), BF16)] * 2,
        compiler_params=pltpu.CompilerParams(
            dimension_semantics=("parallel", "parallel"), vmem_limit_bytes=VMEM_LIMIT),
        name="attention_t",
    )(qt, k, vt)


def _merge_kernel(x_ref, cb_ref, u_ref, uprev_ref, unext_ref, sga_ref, sgb_ref, ot_ref,
                  cw_ref, wco_ref, wao_ref, wo_ref, gt_ref, xo_ref):
    i = pl.program_id(1)
    last = pl.num_programs(1) - 1
    u = u_ref[...].astype(F32)
    tm = u.shape[0]
    halo = uprev_ref.shape[0]
    prev = jnp.where(i > 0, uprev_ref[halo - 1:halo, :].astype(F32), 0.0)
    nxt = jnp.where(i < last, unext_ref[0:1, :].astype(F32), 0.0)
    row = lax.broadcasted_iota(jnp.int32, u.shape, 0)
    u_m = jnp.where(row == 0, prev, pltpu.roll(u, 1, 0))
    u_p = jnp.where(row == tm - 1, nxt, pltpu.roll(u, tm - 1, 0))
    cw = cw_ref[...]
    conv = cw[0:1, :] * u_m + cw[1:2, :] * u + cw[2:3, :] * u_p
    za = (cb_ref[...].astype(F32) * conv).astype(BF16)
    ya = jnp.dot(za, wco_ref[...], preferred_element_type=F32)
    tn = (((0,), (0,)), ((), ()))
    yb = jnp.concatenate(
        [lax.dot_general(ot_ref[j], wao_ref[...], tn, preferred_element_type=F32) for j in range(ot_ref.shape[0])],
        axis=0)
    merged = (sga_ref[...].astype(F32) * ya + sgb_ref[...].astype(F32) * yb).astype(BF16)
    y = jnp.dot(merged, wo_ref[...], preferred_element_type=F32)
    xo_ref[...] = x_ref[...] + gt_ref[...] * y


def _merge_call(layer, x, cb, u, sga, sgb, ot, gt1, wp):
    b, s, _ = x.shape
    lw = lambda *shape: _layer_spec(shape, layer)
    tm = TOKEN_TILE
    halo = 16
    nh = tm // halo
    tile = lambda w: pl.BlockSpec((None, tm, w), lambda bi, i: (bi, i, 0))
    per_batch = pl.BlockSpec((None, 1, D_MODEL), lambda bi, i: (bi, 0, 0))
    in_specs = [
        tile(D_MODEL), tile(CONV_WIDTH), tile(CONV_WIDTH),
        pl.BlockSpec((None, halo, CONV_WIDTH), lambda bi, i: (bi, jnp.maximum(i * nh - 1, 0), 0)),
        pl.BlockSpec((None, halo, CONV_WIDTH), lambda bi, i: (bi, jnp.minimum((i + 1) * nh, s // halo - 1), 0)),
        tile(D_MODEL), tile(D_MODEL),
        pl.BlockSpec((None, tm // ATTN_TILE, N_HEADS * V_DIM, ATTN_TILE), lambda bi, i: (bi, i, 0, 0)),
        lw(3, CONV_WIDTH), lw(CONV_WIDTH, D_MODEL), lw(N_HEADS * V_DIM, D_MODEL), lw(D_MODEL, D_MODEL), per_batch,
    ]
    return pl.pallas_call(
        _merge_kernel, out_shape=jax.ShapeDtypeStruct(x.shape, F32), grid=(b, s // tm),
        in_specs=in_specs, out_specs=tile(D_MODEL),
        compiler_params=pltpu.CompilerParams(
            dimension_semantics=("parallel", "parallel"), vmem_limit_bytes=VMEM_LIMIT),
        name="merge_out_proj",
    )(x, cb, u, u, u, sga, sgb, ot, wp["conv_w"], wp["w_conv_out"], wp["w_attn_out"], wp["w_o"], gt1)


def _route(logits_t, bias_col):
    scores = jax.nn.sigmoid(logits_t)
    biased = scores + bias_col
    neg_inf = jnp.float32(-jnp.inf)
    gscores = []
    for g in range(N_GROUPS):
        rws = [biased[g * GROUP_SIZE + j:g * GROUP_SIZE + j + 1] for j in range(GROUP_SIZE)]
        best = None
        for a in range(GROUP_SIZE):
            for bb in range(a + 1, GROUP_SIZE):
                pair = rws[a] + rws[bb]
                best = pair if best is None else jnp.maximum(best, pair)
        gscores.append(best)
    top = gscores[0]
    sel = jnp.zeros_like(top, dtype=jnp.int32)
    for g in range(1, N_GROUPS):
        better = gscores[g] > top
        sel = jnp.where(better, g, sel)
        top = jnp.where(better, gscores[g], top)
    eidx = lax.broadcasted_iota(jnp.int32, biased.shape, 0)
    masked = jnp.where(eidx // GROUP_SIZE == sel, biased, neg_inf)
    m1 = jnp.max(masked, axis=0, keepdims=True)
    i1 = jnp.min(jnp.where(masked == m1, eidx, N_EXPERTS), axis=0, keepdims=True)
    pick1 = eidx == i1
    masked2 = jnp.where(pick1, neg_inf, masked)
    m2 = jnp.max(masked2, axis=0, keepdims=True)
    i2 = jnp.min(jnp.where(masked2 == m2, eidx, N_EXPERTS), axis=0, keepdims=True)
    pick2 = eidx == i2
    s1 = jnp.sum(jnp.where(pick1, scores, 0.0), axis=0, keepdims=True)
    s2 = jnp.sum(jnp.where(pick2, scores, 0.0), axis=0, keepdims=True)
    return jnp.where(pick1 | pick2, scores / (s1 + s2), 0.0), sel


def _moe_kernel(x_ref, g_ref, sc_ref, sh_ref, gt_ref, wr_ref, br_ref, tri_ref,
                wg_ref, wu_ref, wdn_ref, wsg_ref, wsu_ref, wsdn_ref, xo_ref, hs_ref, cs_ref, ys_ref):
    x = x_ref[...]
    tm = x.shape[0]
    tpad = hs_ref.shape[0]
    r = lax.rsqrt(jnp.mean(x * x, axis=-1, keepdims=True) + EPS)
    h = (x * r) * (g_ref[...] * (1.0 + sc_ref[...])) + sh_ref[...]
    hi, lo = _split_bf16(h)

    nt = (((1,), (1,)), ((), ()))
    l_hi = lax.dot_general(wr_ref[...], hi, nt, preferred_element_type=F32)
    l_lo = lax.dot_general(wr_ref[0:N_EXPERTS, :], lo, nt, preferred_element_type=F32)
    logits_t = l_hi[0:N_EXPERTS] + l_hi[N_EXPERTS:2 * N_EXPERTS] + l_lo
    comb_t, sel = _route(logits_t, br_ref[...])
    comb = jnp.concatenate([comb_t, jnp.zeros((128 - N_EXPERTS, tm), F32)], axis=0).T

    member = (lax.broadcasted_iota(jnp.int32, (8, tm), 0) == sel).astype(F32)
    rank = jnp.dot(member.astype(BF16), tri_ref[...], preferred_element_type=F32)
    count = jnp.sum(member, axis=1, keepdims=True)
    nblk = jnp.floor((count + (ROW_BLOCK - 1)) * (1.0 / ROW_BLOCK))
    start = []
    rows_so_far = jnp.zeros((1, 1), F32)
    for g in range(N_GROUPS):
        start.append(rows_so_far)
        rows_so_far = rows_so_far + nblk[g:g + 1] * ROW_BLOCK
    dest = jnp.zeros((1, tm), F32)
    for g in range(N_GROUPS):
        dest = dest + member[g:g + 1] * (start[g] + rank[g:g + 1])
    perm = (lax.broadcasted_iota(jnp.int32, (tpad, tm), 0) == dest.astype(jnp.int32)).astype(BF16)

    hs_ref[...] = jnp.dot(perm, hi, preferred_element_type=F32).astype(BF16)
    c1 = comb.astype(BF16)
    rest = comb - c1.astype(F32)
    c2, c3 = _split_bf16(rest)
    packed = jnp.concatenate([c1[:, 0:N_EXPERTS], c2[:, 0:N_EXPERTS], c3[:, 0:N_EXPERTS],
                              jnp.zeros((tm, 128 - 3 * N_EXPERTS), BF16)], axis=1)
    cs_ref[...] = jnp.dot(perm, packed, preferred_element_type=F32)
    ys_ref[...] = jnp.zeros(ys_ref.shape, BF16)

    for g in range(N_GROUPS):
        row0 = jnp.sum(start[g]).astype(jnp.int32)
        nb = jnp.sum(nblk[g:g + 1]).astype(jnp.int32)

        def block(blk, carry, g=g, row0=row0):
            rows = pl.ds(pl.multiple_of(row0 + blk * ROW_BLOCK, ROW_BLOCK), ROW_BLOCK)
            hb = hs_ref[rows, :]
            cw = cs_ref[rows, :]
            y = jnp.zeros((ROW_BLOCK, D_MODEL), F32)
            for j in range(GROUP_SIZE):
                e = g * GROUP_SIZE + j
                gate = jnp.dot(hb, wg_ref[e], preferred_element_type=F32)
                up = jnp.dot(hb, wu_ref[e], preferred_element_type=F32)
                w = (cw[:, e:e + 1] + cw[:, N_EXPERTS + e:N_EXPERTS + e + 1]
                     + cw[:, 2 * N_EXPERTS + e:2 * N_EXPERTS + e + 1])
                y = y + jnp.dot((_silu(gate) * up * w).astype(BF16), wdn_ref[e * D_EXPERT:(e + 1) * D_EXPERT, :],
                                preferred_element_type=F32)
            ys_ref[rows, :] = y.astype(BF16)
            return carry

        lax.fori_loop(0, nb, block, 0)

    tn = (((0,), (0,)), ((), ()))
    routed = lax.dot_general(perm, ys_ref[...], tn, preferred_element_type=F32)
    a_s = _silu(jnp.dot(hi, wsg_ref[...], preferred_element_type=F32)) * jnp.dot(
        hi, wsu_ref[...], preferred_element_type=F32)
    y = routed + jnp.dot(a_s.astype(BF16), wsdn_ref[...], preferred_element_type=F32)
    xo_ref[...] = x + gt_ref[...] * y


def _moe_call(layer, x, sc2, sh2, gt2, wp, wr):
    b, s, _ = x.shape
    tm = TOKEN_TILE
    tile = pl.BlockSpec((None, tm, D_MODEL), lambda bi, i: (bi, i, 0))
    per_batch = pl.BlockSpec((None, 1, D_MODEL), lambda bi, i: (bi, 0, 0))
    lw = lambda *shape: _layer_spec(shape, layer)
    tpad = tm + N_GROUPS * ROW_BLOCK
    before = lax.broadcasted_iota(jnp.int32, (tm, tm), 0) < lax.broadcasted_iota(jnp.int32, (tm, tm), 1)
    in_specs = [
        tile, lw(1, D_MODEL), per_batch, per_batch, per_batch,
        _const_spec((2 * N_EXPERTS, D_MODEL)), _const_spec((N_EXPERTS, 1)), _const_spec((tm, tm)),
        lw(N_EXPERTS, D_MODEL, D_EXPERT), lw(N_EXPERTS, D_MODEL, D_EXPERT), lw(N_EXPERTS * D_EXPERT, D_MODEL),
        lw(D_MODEL, D_SHARED), lw(D_MODEL, D_SHARED), lw(D_SHARED, D_MODEL),
    ]
    return pl.pallas_call(
        _moe_kernel, out_shape=jax.ShapeDtypeStruct(x.shape, F32), grid=(b, s // tm),
        in_specs=in_specs, out_specs=tile,
        scratch_shapes=[pltpu.VMEM((tpad, D_MODEL), BF16), pltpu.VMEM((tpad, 128), F32),
                        pltpu.VMEM((tpad, D_MODEL), BF16)],
        compiler_params=pltpu.CompilerParams(
            dimension_semantics=("parallel", "parallel"), vmem_limit_bytes=VMEM_LIMIT),
        name="moe_ffn",
    )(x, wp["g_ffn"], sc2, sh2, gt2, wr["w_hl"], wr["b"], before.astype(BF16), wp["w_e_gate"], wp["w_e_up"],
      wp["w_e_down"], wp["w_s_gate"], wp["w_s_up"], wp["w_s_down"])


def _prep_weights(g_mix, g_ffn, w_in, conv_w, g_q_a, w_uq, g_kv_a, w_ukv, g_qn, g_kn, w_conv_out, w_attn_out,
                  w_o, w_e_gate, w_e_up, w_e_down, w_s_gate, w_s_up, w_s_down):
    depth = w_in.shape[0]
    split = 3 * CONV_WIDTH + Q_RANK + KV_RANK
    w_pe = w_in[:, :, split:split + QK_ROPE]
    w_pe_sw = jnp.concatenate([w_pe[:, :, ROPE_HALF:], w_pe[:, :, :ROPE_HALF]], axis=2)
    lane_pad = lambda w: jnp.pad(w, ((0, 0), (0, 0), (QK_NOPE, HEAD_PAD - QK_DIM)))
    w_in_r = jnp.concatenate(
        [w_in[:, :, :split], lane_pad(w_pe), lane_pad(w_pe_sw), w_in[:, :, split + QK_ROPE:]], axis=2).astype(BF16)

    wq = w_uq.reshape(depth, Q_RANK, N_HEADS, QK_DIM)
    wq = jnp.pad(wq, ((0, 0), (0, 0), (0, 0), (0, HEAD_PAD - QK_DIM))).reshape(depth, Q_RANK, N_HEADS * HEAD_PAD)
    wkv = w_ukv.reshape(depth, KV_RANK, N_HEADS, QK_NOPE + V_DIM)
    wk = jnp.pad(wkv[..., :QK_NOPE], ((0, 0), (0, 0), (0, 0), (0, HEAD_PAD - QK_NOPE)))
    wk = wk.reshape(depth, KV_RANK, N_HEADS * HEAD_PAD)
    wv = wkv[..., QK_NOPE:].reshape(depth, KV_RANK, N_HEADS * V_DIM)
    scale = QK_DIM ** -0.5 * LOG2_E
    g_q = jnp.pad(g_qn * scale, ((0, 0), (0, HEAD_PAD - QK_DIM))).reshape(depth, HEAD_PAD, 1)
    g_k = jnp.pad(g_kn, ((0, 0), (0, HEAD_PAD - QK_DIM))).reshape(depth, 1, HEAD_PAD)
    g_rope = g_kn[:, QK_NOPE:]
    g_k_sw = jnp.pad(jnp.concatenate([g_rope[:, ROPE_HALF:], g_rope[:, :ROPE_HALF]], axis=1),
                     ((0, 0), (QK_NOPE, HEAD_PAD - QK_DIM))).reshape(depth, 1, HEAD_PAD)
    return {
        "g_mix": g_mix.reshape(depth, 1, D_MODEL),
        "g_ffn": g_ffn.reshape(depth, 1, D_MODEL),
        "w_in": w_in_r,
        "g_q_a": g_q_a.reshape(depth, 1, Q_RANK),
        "w_uq_t": wq.transpose(0, 2, 1).astype(BF16),
        "g_kv_a": g_kv_a.reshape(depth, 1, KV_RANK),
        "w_uk": wk.astype(BF16),
        "w_uv_t": wv.transpose(0, 2, 1).astype(BF16),
        "g_q": g_q, "g_k": g_k, "g_k_sw": g_k_sw,
        "conv_w": conv_w,
        "w_conv_out": w_conv_out.astype(BF16),
        "w_attn_out": w_attn_out.astype(BF16),
        "w_o": w_o.astype(BF16),
        "w_e_gate": w_e_gate.astype(BF16),
        "w_e_up": w_e_up.astype(BF16),
        "w_e_down": w_e_down.astype(BF16).reshape(depth, N_EXPERTS * D_EXPERT, D_MODEL),
        "w_s_gate": w_s_gate.astype(BF16),
        "w_s_up": w_s_up.astype(BF16),
        "w_s_down": w_s_down.astype(BF16),
    }


def _rope_tables(positions):
    inv_freq = ROPE_THETA ** (-jnp.arange(ROPE_HALF, dtype=F32) / ROPE_HALF)
    ang_t = inv_freq[None, :, None] * positions.astype(F32)[:, None, :]
    row = jnp.arange(ROPE_HALF)[:, None]
    lane = jnp.arange(HEAD_PAD)[None, :]
    first = lane == row + QK_NOPE
    second = lane == row + QK_NOPE + ROPE_HALF
    return {
        "cos_t": jnp.cos(ang_t), "sin_t": jnp.sin(ang_t),
        "e_cos": (first | second).astype(BF16),
        "e_sin": (second.astype(F32) - first.astype(F32)).astype(BF16),
    }


def kernel(x, c, positions, w_ada, b_ada, g_mix, g_ffn, w_in, conv_w, g_q_a, w_uq, g_kv_a, w_ukv,
           g_qn, g_kn, w_conv_out, w_attn_out, w_o, w_router, b_router, w_e_gate, w_e_up, w_e_down,
           w_s_gate, w_s_up, w_s_down):
    depth = w_ada.shape[0]
    b = x.shape[0]
    c_pad = jnp.pad(c, ((0, 8 - b), (0, 0)))
    mod = _modulation(c_pad, w_ada, b_ada)[:, :b, :]
    mod = mod.reshape(depth, b, 6, 1, D_MODEL)

    tabs = _rope_tables(positions)
    wr_hi, wr_lo = _split_bf16(w_router.T)
    wr = {"w_hl": jnp.concatenate([wr_hi, wr_lo], axis=0), "b": b_router.reshape(N_EXPERTS, 1)}
    wp = _prep_weights(g_mix, g_ffn, w_in, conv_w, g_q_a, w_uq, g_kv_a, w_ukv, g_qn, g_kn, w_conv_out,
                       w_attn_out, w_o, w_e_gate, w_e_up, w_e_down, w_s_gate, w_s_up, w_s_down)

    for l in range(depth):
        sh1, sc1, gt1, sh2, sc2, gt2 = (mod[l, :, j] for j in range(6))
        cb, u, sga, sgb, qt, k, vt = _proj_call(l, x, sc1, sh1, wp, tabs)
        ot = _attn_call(qt, k, vt)
        x = _merge_call(l, x, cb, u, sga, sgb, ot, gt1, wp)
        x = _moe_call(l, x, sc2, sh2, gt2, wp, wr)
    return x
```

```python
import functools

import jax
import jax.numpy as jnp
from jax import lax
from jax.experimental import pallas as pl
from jax.experimental.pallas import tpu as pltpu

F32 = jnp.float32
BF16 = jnp.bfloat16

D_MODEL = 1024
CONV_WIDTH = 512
N_HEADS = 8
QK_NOPE = 64
QK_ROPE = 32
V_DIM = 64
QK_DIM = QK_NOPE + QK_ROPE
Q_RANK = 256
KV_RANK = 128
ROPE_THETA = 10000.0
N_EXPERTS = 16
N_GROUPS = 4
GROUP_SIZE = N_EXPERTS // N_GROUPS
D_EXPERT = 256
D_SHARED = 256
EPS = 1e-6

HEAD_PAD = 128
V_ROWS = 80
LAT_START = 3 * CONV_WIDTH
ROPE_HALF = QK_ROPE // 2

TOKEN_TILE = 512
ATTN_TILE = 256
ROW_BLOCK = 128
LOG2_E = 1.4426950408889634
VMEM_LIMIT = 56 * 1024 * 1024
NEG_BIG = -1e30


def _const_spec(shape):
    nd = len(shape)
    return pl.BlockSpec(shape, lambda *_: (0,) * nd, pipeline_mode=pl.Buffered(1))


def _layer_spec(shape, layer):
    nd = len(shape)
    return pl.BlockSpec((None,) + tuple(shape), lambda *_: (layer,) + (0,) * nd, pipeline_mode=pl.Buffered(1))


def _silu(v):
    return v * jax.nn.sigmoid(v)


def _split_bf16(v):
    hi = v.astype(BF16)
    return hi, (v - hi.astype(F32)).astype(BF16)


def _mod_kernel(c_ref, w_ref, b_ref, o_ref):
    hi, lo = _split_bf16(_silu(c_ref[...]))
    w = w_ref[...].astype(BF16)
    o_ref[...] = (jnp.dot(hi, w, preferred_element_type=F32) + jnp.dot(lo, w, preferred_element_type=F32)
                  + b_ref[...])


def _modulation(c_pad, w_ada, b_ada):
    depth = w_ada.shape[0]
    nblk = w_ada.shape[2] // D_MODEL
    rows = c_pad.shape[0]
    return pl.pallas_call(
        _mod_kernel,
        out_shape=jax.ShapeDtypeStruct((depth, rows, w_ada.shape[2]), F32),
        grid=(depth, nblk),
        in_specs=[
            pl.BlockSpec((rows, D_MODEL), lambda l, j: (0, 0)),
            pl.BlockSpec((None, D_MODEL, D_MODEL), lambda l, j: (l, 0, j)),
            pl.BlockSpec((None, 1, D_MODEL), lambda l, j: (l, 0, j)),
        ],
        out_specs=pl.BlockSpec((None, rows, D_MODEL), lambda l, j: (l, 0, j)),
        compiler_params=pltpu.CompilerParams(
            dimension_semantics=("arbitrary", "arbitrary"), vmem_limit_bytes=VMEM_LIMIT),
        name="adaln_mod",
    )(c_pad, w_ada, b_ada.reshape(depth, 1, -1))


def _store_token_tiles(ref, row0, val):
    rows = val.shape[0]
    for j in range(val.shape[1] // ATTN_TILE):
        ref[j, row0:row0 + rows, :] = val[:, j * ATTN_TILE:(j + 1) * ATTN_TILE]


def _proj_kernel(x_ref, g_ref, sc_ref, sh_ref, wmain_ref, wpe_ref, wgate_ref, gqa_ref, wuqt_ref, gkva_ref, wuk_ref,
                 wuvt_ref, gq_ref, gk_ref, gksw_ref, ones_ref, ecos_ref, esin_ref, cost_ref, sint_ref,
                 cb_ref, u_ref, sga_ref, sgb_ref, qt_ref, k_ref, vt_ref):
    x = x_ref[...]
    r = lax.rsqrt(jnp.mean(x * x, axis=-1, keepdims=True) + EPS)
    h = (x * r) * (g_ref[...] * (1.0 + sc_ref[...])) + sh_ref[...]
    hb = h.astype(BF16)
    tm = x.shape[0]

    lat = jnp.dot(hb, wmain_ref[:, LAT_START:LAT_START + Q_RANK + KV_RANK], preferred_element_type=F32)
    cq = lat[:, 0:Q_RANK]
    ckv = lat[:, Q_RANK:Q_RANK + KV_RANK]
    pe = jnp.dot(hb, wpe_ref[...], preferred_element_type=F32)
    kpe = pe[:, 0:HEAD_PAD]
    kpe_sw = pe[:, HEAD_PAD:2 * HEAD_PAD]

    cqn = (cq * lax.rsqrt(jnp.mean(cq * cq, axis=-1, keepdims=True) + EPS) * gqa_ref[...]).astype(BF16)
    ckvn = (ckv * lax.rsqrt(jnp.mean(ckv * ckv, axis=-1, keepdims=True) + EPS) * gkva_ref[...]).astype(BF16)

    nt = (((1,), (1,)), ((), ()))
    qt = lax.dot_general(wuqt_ref[...], cqn, nt, preferred_element_type=F32)
    cost = cost_ref[...]
    sint = sint_ref[...]
    gq = gq_ref[...]
    for hh in range(N_HEADS):
        base = hh * HEAD_PAD
        qh = qt[base:base + HEAD_PAD]
        rr = lax.rsqrt(jnp.sum(qh * qh, axis=0, keepdims=True) * (1.0 / QK_DIM) + EPS)
        qn = qh * rr * gq
        x1 = qn[QK_NOPE:QK_NOPE + ROPE_HALF]
        x2 = qn[QK_NOPE + ROPE_HALF:QK_DIM]
        _store_token_tiles(qt_ref, base, qn[0:QK_NOPE].astype(BF16))
        _store_token_tiles(qt_ref, base + QK_NOPE, (x1 * cost - x2 * sint).astype(BF16))
        _store_token_tiles(qt_ref, base + QK_NOPE + ROPE_HALF, (x2 * cost + x1 * sint).astype(BF16))
        _store_token_tiles(qt_ref, base + QK_DIM, jnp.zeros((HEAD_PAD - QK_DIM, tm), BF16))

    kn = jnp.dot(ckvn, wuk_ref[...], preferred_element_type=F32)
    tn = (((0,), (0,)), ((), ()))
    place = lambda tab, e: sum(lax.dot_general(part, e, tn, preferred_element_type=F32) for part in _split_bf16(tab))
    lane = lax.broadcasted_iota(jnp.int32, (1, HEAD_PAD), 1)
    ck = jnp.where(lane < QK_NOPE, 1.0, 0.0) + place(cost, ecos_ref[...])
    ss = place(sint, esin_ref[...])
    direct = gk_ref[...] * ck
    partner = kpe_sw * (gksw_ref[...] * ss)
    ones = ones_ref[...]
    for hh in range(N_HEADS):
        base = hh * HEAD_PAD
        kh = kn[:, base:base + HEAD_PAD] + kpe
        ssq = jnp.dot((kh * kh).astype(BF16), ones, preferred_element_type=F32)
        rr = lax.rsqrt(ssq * (1.0 / QK_DIM) + EPS)
        k_ref[:, base:base + HEAD_PAD] = (rr * (kh * direct + partner)).astype(BF16)

    vt = lax.dot_general(wuvt_ref[...], ckvn, nt, preferred_element_type=F32)
    for hh in range(N_HEADS):
        _store_token_tiles(vt_ref, hh * V_ROWS, vt[hh * V_DIM:(hh + 1) * V_DIM].astype(BF16))
        _store_token_tiles(vt_ref, hh * V_ROWS + V_DIM, jnp.ones((V_ROWS - V_DIM, tm), BF16))

    pc = jnp.dot(hb, wmain_ref[:, 0:LAT_START], preferred_element_type=F32)
    cb_ref[...] = pc[:, 0:CONV_WIDTH].astype(BF16)
    u_ref[...] = (pc[:, CONV_WIDTH:2 * CONV_WIDTH] * pc[:, 2 * CONV_WIDTH:3 * CONV_WIDTH]).astype(BF16)

    pg = jnp.dot(hb, wgate_ref[...], preferred_element_type=F32)
    sga_ref[...] = jax.nn.sigmoid(pg[:, 0:D_MODEL]).astype(BF16)
    sgb_ref[...] = jax.nn.sigmoid(pg[:, D_MODEL:2 * D_MODEL]).astype(BF16)


def _proj_call(layer, x, sc1, sh1, wp, tabs):
    b, s, _ = x.shape
    tm = TOKEN_TILE
    at = ATTN_TILE
    tile = lambda w: pl.BlockSpec((None, tm, w), lambda bi, i: (bi, i, 0))
    tile_t = lambda rws: pl.BlockSpec((None, rws, tm), lambda bi, i: (bi, 0, i))
    per_batch = pl.BlockSpec((None, 1, D_MODEL), lambda bi, i: (bi, 0, 0))
    lw = lambda *shape: _layer_spec(shape, layer)
    in_specs = [
        tile(D_MODEL), lw(1, D_MODEL), per_batch, per_batch,
        lw(D_MODEL, LAT_START + Q_RANK + KV_RANK), lw(D_MODEL, 2 * HEAD_PAD), lw(D_MODEL, 2 * D_MODEL), lw(1, Q_RANK),
        lw(N_HEADS * HEAD_PAD, Q_RANK), lw(1, KV_RANK),
        lw(KV_RANK, N_HEADS * HEAD_PAD), lw(N_HEADS * V_DIM, KV_RANK),
        lw(HEAD_PAD, 1), lw(1, HEAD_PAD), lw(1, HEAD_PAD),
        _const_spec((HEAD_PAD, HEAD_PAD)), _const_spec((ROPE_HALF, HEAD_PAD)), _const_spec((ROPE_HALF, HEAD_PAD)),
        tile_t(ROPE_HALF), tile_t(ROPE_HALF),
    ]
    out_shape = [
        jax.ShapeDtypeStruct((b, s, CONV_WIDTH), BF16),
        jax.ShapeDtypeStruct((b, s, CONV_WIDTH), BF16),
        jax.ShapeDtypeStruct((b, s, D_MODEL), BF16),
        jax.ShapeDtypeStruct((b, s, D_MODEL), BF16),
        jax.ShapeDtypeStruct((b, s // at, N_HEADS * HEAD_PAD, at), BF16),
        jax.ShapeDtypeStruct((b, s, N_HEADS * HEAD_PAD), BF16),
        jax.ShapeDtypeStruct((b, s // at, N_HEADS * V_ROWS, at), BF16),
    ]
    att_tiles = lambda rws: pl.BlockSpec((None, tm // at, rws, at), lambda bi, i: (bi, i, 0, 0))
    out_specs = [tile(CONV_WIDTH), tile(CONV_WIDTH), tile(D_MODEL), tile(D_MODEL),
                 att_tiles(N_HEADS * HEAD_PAD), tile(N_HEADS * HEAD_PAD), att_tiles(N_HEADS * V_ROWS)]
    return pl.pallas_call(
        _proj_kernel, out_shape=out_shape, grid=(b, s // tm),
        in_specs=in_specs, out_specs=out_specs,
        compiler_params=pltpu.CompilerParams(
            dimension_semantics=("parallel", "parallel"), vmem_limit_bytes=VMEM_LIMIT),
        name="norm_in_proj",
    )(x, wp["g_mix"], sc1, sh1, wp["w_in_main"], wp["w_in_pe"], wp["w_in_gate"], wp["g_q_a"], wp["w_uq_t"], wp["g_kv_a"], wp["w_uk"],
      wp["w_uv_t"], wp["g_q"], wp["g_k"], wp["g_k_sw"], jnp.ones((HEAD_PAD, HEAD_PAD), BF16),
      tabs["e_cos"], tabs["e_sin"], tabs["cos_t"], tabs["sin_t"])


def _attn_kernel(qt_ref, k_ref, vt_ref, ot_ref, s0, s1, p0, p1, cm0, cm1, al0, al1):
    s_bufs = (s0, s1)
    p_bufs = (p0, p1)
    cmax_bufs = (cm0, cm1)
    alpha_bufs = (al0, al1)
    nq, _, tq = qt_ref.shape
    nk, _, tk = vt_ref.shape

    def score_chunk(qt, c, dst):
        sc = jnp.dot(k_ref[c * tk:(c + 1) * tk, :], qt, preferred_element_type=F32)
        dst[c] = sc
        return jnp.max(sc, axis=0, keepdims=True)

    def step(i, cur, with_scores=True):
        other = 1 - cur
        qt_next = qt_ref[i + 1] if with_scores else None
        m = jnp.full((1, tq), NEG_BIG, F32)
        acc = jnp.zeros((V_ROWS, tq), F32)
        for c in range(nk):
            m_new = jnp.maximum(m, cmax_bufs[cur][c:c + 1, :])
            alpha_bufs[cur][c:c + 1, :] = jnp.exp2(m - m_new)
            p_bufs[cur][c] = jnp.exp2(s_bufs[cur][c] - m_new).astype(BF16)
            m = m_new
            if with_scores:
                cmax_bufs[other][c:c + 1, :] = score_chunk(qt_next, c, s_bufs[other])
            acc = alpha_bufs[other][c:c + 1, :] * acc + jnp.dot(
                vt_ref[c], p_bufs[other][c], preferred_element_type=F32)
        ot_ref[jnp.maximum(i - 1, 0)] = (acc[0:V_DIM] / acc[V_DIM:V_DIM + 1]).astype(BF16)

    def trip(t, carry):
        step(2 * t, 0)
        step(2 * t + 1, 1)
        return carry

    qt0 = qt_ref[0]
    for c in range(nk):
        cm0[c:c + 1, :] = score_chunk(qt0, c, s0)
        p1[c] = jnp.ones((tk, tq), BF16)
    al1[...] = jnp.ones((nk, tq), F32)
    lax.fori_loop(0, nq // 2 - 1, trip, 0)
    step(nq - 2, 0)
    step(nq - 1, 1, with_scores=False)
    acc = jnp.zeros((V_ROWS, tq), F32)
    for c in range(nk):
        acc = al1[c:c + 1, :] * acc + jnp.dot(vt_ref[c], p1[c], preferred_element_type=F32)
    ot_ref[nq - 1] = (acc[0:V_DIM] / acc[V_DIM:V_DIM + 1]).astype(BF16)


def _attn_call(qt, k, vt):
    b, nq, _, tq = qt.shape
    _, nk, _, tk = vt.shape
    s = k.shape[1]
    assert nq % 2 == 0 and nq >= 4 and nk * tk == s
    return pl.pallas_call(
        _attn_kernel,
        out_shape=jax.ShapeDtypeStruct((b, nq, N_HEADS * V_DIM, tq), BF16),
        grid=(b, N_HEADS),
        in_specs=[
            pl.BlockSpec((None, nq, HEAD_PAD, tq), lambda bi, hh: (bi, 0, hh, 0)),
            pl.BlockSpec((None, s, HEAD_PAD), lambda bi, hh: (bi, 0, hh)),
            pl.BlockSpec((None, nk, V_ROWS, tk), lambda bi, hh: (bi, 0, hh, 0)),
        ],
        out_specs=pl.BlockSpec((None, nq, V_DIM, tq), lambda bi, hh: (bi, 0, hh, 0)),
        scratch_shapes=([pltpu.VMEM((nk, tk, tq), F32)] * 2 + [pltpu.VMEM((nk, tk, tq), BF16)] * 2
                        + [pltpu.VMEM((nk, tq), F32)] * 4),
        compiler_params=pltpu.CompilerParams(
            dimension_semantics=("parallel", "parallel"), vmem_limit_bytes=VMEM_LIMIT),
        name="attention_t",
    )(qt, k, vt)


def _merge_kernel(x_ref, cb_ref, u_ref, uprev_ref, unext_ref, sga_ref, sgb_ref, ot_ref,
                  cw_ref, wco_ref, wao_ref, wo_ref, gt_ref, xo_ref):
    i = pl.program_id(1)
    last = pl.num_programs(1) - 1
    u = u_ref[...].astype(F32)
    tm = u.shape[0]
    halo = uprev_ref.shape[0]
    prev = jnp.where(i > 0, uprev_ref[halo - 1:halo, :].astype(F32), 0.0)
    nxt = jnp.where(i < last, unext_ref[0:1, :].astype(F32), 0.0)
    row = lax.broadcasted_iota(jnp.int32, u.shape, 0)
    u_m = jnp.where(row == 0, prev, pltpu.roll(u, 1, 0))
    u_p = jnp.where(row == tm - 1, nxt, pltpu.roll(u, tm - 1, 0))
    cw = cw_ref[...]
    conv = cw[0:1, :] * u_m + cw[1:2, :] * u + cw[2:3, :] * u_p
    za = (cb_ref[...].astype(F32) * conv).astype(BF16)
    ya = jnp.dot(za, wco_ref[...], preferred_element_type=F32)
    tn = (((0,), (0,)), ((), ()))
    yb = jnp.concatenate(
        [lax.dot_general(ot_ref[j], wao_ref[...], tn, preferred_element_type=F32) for j in range(ot_ref.shape[0])],
        axis=0)
    merged = (sga_ref[...].astype(F32) * ya + sgb_ref[...].astype(F32) * yb).astype(BF16)
    y = jnp.dot(merged, wo_ref[...], preferred_element_type=F32)
    xo_ref[...] = x_ref[...] + gt_ref[...] * y


def _merge_call(layer, x, cb, u, sga, sgb, ot, gt1, wp):
    b, s, _ = x.shape
    lw = lambda *shape: _layer_spec(shape, layer)
    tm = TOKEN_TILE
    halo = 16
    nh = tm // halo
    tile = lambda w: pl.BlockSpec((None, tm, w), lambda bi, i: (bi, i, 0))
    per_batch = pl.BlockSpec((None, 1, D_MODEL), lambda bi, i: (bi, 0, 0))
    in_specs = [
        tile(D_MODEL), tile(CONV_WIDTH), tile(CONV_WIDTH),
        pl.BlockSpec((None, halo, CONV_WIDTH), lambda bi, i: (bi, jnp.maximum(i * nh - 1, 0), 0)),
        pl.BlockSpec((None, halo, CONV_WIDTH), lambda bi, i: (bi, jnp.minimum((i + 1) * nh, s // halo - 1), 0)),
        tile(D_MODEL), tile(D_MODEL),
        pl.BlockSpec((None, tm // ATTN_TILE, N_HEADS * V_DIM, ATTN_TILE), lambda bi, i: (bi, i, 0, 0)),
        lw(3, CONV_WIDTH), lw(CONV_WIDTH, D_MODEL), lw(N_HEADS * V_DIM, D_MODEL), lw(D_MODEL, D_MODEL), per_batch,
    ]
    return pl.pallas_call(
        _merge_kernel, out_shape=jax.ShapeDtypeStruct(x.shape, F32), grid=(b, s // tm),
        in_specs=in_specs, out_specs=tile(D_MODEL),
        compiler_params=pltpu.CompilerParams(
            dimension_semantics=("parallel", "parallel"), vmem_limit_bytes=VMEM_LIMIT),
        name="merge_out_proj",
    )(x, cb, u, u, u, sga, sgb, ot, wp["conv_w"], wp["w_conv_out"], wp["w_attn_out"], wp["w_o"], gt1)


def _route(logits_t, bias_col):
    scores = jax.nn.sigmoid(logits_t)
    biased = scores + bias_col
    neg_inf = jnp.float32(-jnp.inf)
    gscores = []
    for g in range(N_GROUPS):
        rws = [biased[g * GROUP_SIZE + j:g * GROUP_SIZE + j + 1] for j in range(GROUP_SIZE)]
        best = None
        for a in range(GROUP_SIZE):
            for bb in range(a + 1, GROUP_SIZE):
                pair = rws[a] + rws[bb]
                best = pair if best is None else jnp.maximum(best, pair)
        gscores.append(best)
    top = gscores[0]
    sel = jnp.zeros_like(top, dtype=jnp.int32)
    for g in range(1, N_GROUPS):
        better = gscores[g] > top
        sel = jnp.where(better, g, sel)
        top = jnp.where(better, gscores[g], top)
    eidx = lax.broadcasted_iota(jnp.int32, biased.shape, 0)
    masked = jnp.where(eidx // GROUP_SIZE == sel, biased, neg_inf)
    m1 = jnp.max(masked, axis=0, keepdims=True)
    i1 = jnp.min(jnp.where(masked == m1, eidx, N_EXPERTS), axis=0, keepdims=True)
    pick1 = eidx == i1
    masked2 = jnp.where(pick1, neg_inf, masked)
    m2 = jnp.max(masked2, axis=0, keepdims=True)
    i2 = jnp.min(jnp.where(masked2 == m2, eidx, N_EXPERTS), axis=0, keepdims=True)
    pick2 = eidx == i2
    s1 = jnp.sum(jnp.where(pick1, scores, 0.0), axis=0, keepdims=True)
    s2 = jnp.sum(jnp.where(pick2, scores, 0.0), axis=0, keepdims=True)
    return jnp.where(pick1 | pick2, scores / (s1 + s2), 0.0), sel


def _moe_kernel(x_ref, g_ref, sc_ref, sh_ref, gt_ref, wr_ref, br_ref, tri_ref,
                wg_ref, wu_ref, wdn_ref, wsg_ref, wsu_ref, wsdn_ref, xo_ref, hs_ref, cs_ref, ys_ref):
    x = x_ref[...]
    tm = x.shape[0]
    tpad = hs_ref.shape[0]
    r = lax.rsqrt(jnp.mean(x * x, axis=-1, keepdims=True) + EPS)
    h = (x * r) * (g_ref[...] * (1.0 + sc_ref[...])) + sh_ref[...]
    hi, lo = _split_bf16(h)

    nt = (((1,), (1,)), ((), ()))
    l_hi = lax.dot_general(wr_ref[...], hi, nt, preferred_element_type=F32)
    l_lo = lax.dot_general(wr_ref[0:N_EXPERTS, :], lo, nt, preferred_element_type=F32)
    logits_t = l_hi[0:N_EXPERTS] + l_hi[N_EXPERTS:2 * N_EXPERTS] + l_lo
    comb_t, sel = _route(logits_t, br_ref[...])
    comb = jnp.concatenate([comb_t, jnp.zeros((128 - N_EXPERTS, tm), F32)], axis=0).T

    member = (lax.broadcasted_iota(jnp.int32, (8, tm), 0) == sel).astype(F32)
    rank = jnp.dot(member.astype(BF16), tri_ref[...], preferred_element_type=F32)
    count = jnp.sum(member, axis=1, keepdims=True)
    nblk = jnp.floor((count + (ROW_BLOCK - 1)) * (1.0 / ROW_BLOCK))
    start = []
    rows_so_far = jnp.zeros((1, 1), F32)
    for g in range(N_GROUPS):
        start.append(rows_so_far)
        rows_so_far = rows_so_far + nblk[g:g + 1] * ROW_BLOCK
    dest = jnp.zeros((1, tm), F32)
    for g in range(N_GROUPS):
        dest = dest + member[g:g + 1] * (start[g] + rank[g:g + 1])
    perm = (lax.broadcasted_iota(jnp.int32, (tpad, tm), 0) == dest.astype(jnp.int32)).astype(BF16)

    hs_ref[...] = jnp.dot(perm, hi, preferred_element_type=F32).astype(BF16)
    c1 = comb.astype(BF16)
    rest = comb - c1.astype(F32)
    c2, c3 = _split_bf16(rest)
    packed = jnp.concatenate([c1[:, 0:N_EXPERTS], c2[:, 0:N_EXPERTS], c3[:, 0:N_EXPERTS],
                              jnp.zeros((tm, 128 - 3 * N_EXPERTS), BF16)], axis=1)
    cs_ref[...] = jnp.dot(perm, packed, preferred_element_type=F32)
    ys_ref[...] = jnp.zeros(ys_ref.shape, BF16)

    for g in range(N_GROUPS):
        row0 = jnp.sum(start[g]).astype(jnp.int32)
        nb = jnp.sum(nblk[g:g + 1]).astype(jnp.int32)

        def block(blk, carry, g=g, row0=row0):
            rows = pl.ds(pl.multiple_of(row0 + blk * ROW_BLOCK, ROW_BLOCK), ROW_BLOCK)
            hb = hs_ref[rows, :]
            cw = cs_ref[rows, :]
            y = jnp.zeros((ROW_BLOCK, D_MODEL), F32)
            for j in range(GROUP_SIZE):
                e = g * GROUP_SIZE + j
                gate = jnp.dot(hb, wg_ref[e], preferred_element_type=F32)
                up = jnp.dot(hb, wu_ref[e], preferred_element_type=F32)
                w = (cw[:, e:e + 1] + cw[:, N_EXPERTS + e:N_EXPERTS + e + 1]
                     + cw[:, 2 * N_EXPERTS + e:2 * N_EXPERTS + e + 1])
                y = y + jnp.dot((_silu(gate) * up * w).astype(BF16), wdn_ref[e * D_EXPERT:(e + 1) * D_EXPERT, :],
                                preferred_element_type=F32)
            ys_ref[rows, :] = y.astype(BF16)
            return carry

        lax.fori_loop(0, nb, block, 0)

    tn = (((0,), (0,)), ((), ()))
    routed = lax.dot_general(perm, ys_ref[...], tn, preferred_element_type=F32)
    a_s = _silu(jnp.dot(hi, wsg_ref[...], preferred_element_type=F32)) * jnp.dot(
        hi, wsu_ref[...], preferred_element_type=F32)
    y = routed + jnp.dot(a_s.astype(BF16), wsdn_ref[...], preferred_element_type=F32)
    xo_ref[...] = x + gt_ref[...] * y


def _moe_call(layer, x, sc2, sh2, gt2, wp, wr):
    b, s, _ = x.shape
    tm = TOKEN_TILE
    tile = pl.BlockSpec((None, tm, D_MODEL), lambda bi, i: (bi, i, 0))
    per_batch = pl.BlockSpec((None, 1, D_MODEL), lambda bi, i: (bi, 0, 0))
    lw = lambda *shape: _layer_spec(shape, layer)
    tpad = tm + N_GROUPS * ROW_BLOCK
    before = lax.broadcasted_iota(jnp.int32, (tm, tm), 0) < lax.broadcasted_iota(jnp.int32, (tm, tm), 1)
    in_specs = [
        tile, lw(1, D_MODEL), per_batch, per_batch, per_batch,
        _const_spec((2 * N_EXPERTS, D_MODEL)), _const_spec((N_EXPERTS, 1)), _const_spec((tm, tm)),
        lw(N_EXPERTS, D_MODEL, D_EXPERT), lw(N_EXPERTS, D_MODEL, D_EXPERT), lw(N_EXPERTS * D_EXPERT, D_MODEL),
        lw(D_MODEL, D_SHARED), lw(D_MODEL, D_SHARED), lw(D_SHARED, D_MODEL),
    ]
    return pl.pallas_call(
        _moe_kernel, out_shape=jax.ShapeDtypeStruct(x.shape, F32), grid=(b, s // tm),
        in_specs=in_specs, out_specs=tile,
        scratch_shapes=[pltpu.VMEM((tpad, D_MODEL), BF16), pltpu.VMEM((tpad, 128), F32),
                        pltpu.VMEM((tpad, D_MODEL), BF16)],
        compiler_params=pltpu.CompilerParams(
            dimension_semantics=("parallel", "parallel"), vmem_limit_bytes=VMEM_LIMIT),
        name="moe_ffn",
    )(x, wp["g_ffn"], sc2, sh2, gt2, wr["w_hl"], wr["b"], before.astype(BF16), wp["w_e_gate"], wp["w_e_up"],
      wp["w_e_down"], wp["w_s_gate"], wp["w_s_up"], wp["w_s_down"])


def _prep_weights(g_mix, g_ffn, w_in, conv_w, g_q_a, w_uq, g_kv_a, w_ukv, g_qn, g_kn, w_conv_out, w_attn_out,
                  w_o, w_e_gate, w_e_up, w_e_down, w_s_gate, w_s_up, w_s_down):
    depth = w_in.shape[0]
    split = 3 * CONV_WIDTH + Q_RANK + KV_RANK
    w_pe = w_in[:, :, split:split + QK_ROPE]
    w_pe_sw = jnp.concatenate([w_pe[:, :, ROPE_HALF:], w_pe[:, :, :ROPE_HALF]], axis=2)
    lane_pad = lambda w: jnp.pad(w, ((0, 0), (0, 0), (QK_NOPE, HEAD_PAD - QK_DIM)))
    w_in_pe = jnp.concatenate([lane_pad(w_pe), lane_pad(w_pe_sw)], axis=2).astype(BF16)

    wq = w_uq.reshape(depth, Q_RANK, N_HEADS, QK_DIM)
    wq = jnp.pad(wq, ((0, 0), (0, 0), (0, 0), (0, HEAD_PAD - QK_DIM))).reshape(depth, Q_RANK, N_HEADS * HEAD_PAD)
    wkv = w_ukv.reshape(depth, KV_RANK, N_HEADS, QK_NOPE + V_DIM)
    wk = jnp.pad(wkv[..., :QK_NOPE], ((0, 0), (0, 0), (0, 0), (0, HEAD_PAD - QK_NOPE)))
    wk = wk.reshape(depth, KV_RANK, N_HEADS * HEAD_PAD)
    wv = wkv[..., QK_NOPE:].reshape(depth, KV_RANK, N_HEADS * V_DIM)
    scale = QK_DIM ** -0.5 * LOG2_E
    g_q = jnp.pad(g_qn * scale, ((0, 0), (0, HEAD_PAD - QK_DIM))).reshape(depth, HEAD_PAD, 1)
    g_k = jnp.pad(g_kn, ((0, 0), (0, HEAD_PAD - QK_DIM))).reshape(depth, 1, HEAD_PAD)
    g_rope = g_kn[:, QK_NOPE:]
    g_k_sw = jnp.pad(jnp.concatenate([g_rope[:, ROPE_HALF:], g_rope[:, :ROPE_HALF]], axis=1),
                     ((0, 0), (QK_NOPE, HEAD_PAD - QK_DIM))).reshape(depth, 1, HEAD_PAD)
    return {
        "g_mix": g_mix.reshape(depth, 1, D_MODEL),
        "g_ffn": g_ffn.reshape(depth, 1, D_MODEL),
        "w_in_main": w_in[:, :, :split].astype(BF16),
        "w_in_pe": w_in_pe,
        "w_in_gate": w_in[:, :, split + QK_ROPE:].astype(BF16),
        "g_q_a": g_q_a.reshape(depth, 1, Q_RANK),
        "w_uq_t": wq.transpose(0, 2, 1).astype(BF16),
        "g_kv_a": g_kv_a.reshape(depth, 1, KV_RANK),
        "w_uk": wk.astype(BF16),
        "w_uv_t": wv.transpose(0, 2, 1).astype(BF16),
        "g_q": g_q, "g_k": g_k, "g_k_sw": g_k_sw,
        "conv_w": conv_w,
        "w_conv_out": w_conv_out.astype(BF16),
        "w_attn_out": w_attn_out.astype(BF16),
        "w_o": w_o.astype(BF16),
        "w_e_gate": w_e_gate.astype(BF16),
        "w_e_up": w_e_up.astype(BF16),
        "w_e_down": w_e_down.astype(BF16).reshape(depth, N_EXPERTS * D_EXPERT, D_MODEL),
        "w_s_gate": w_s_gate.astype(BF16),
        "w_s_up": w_s_up.astype(BF16),
        "w_s_down": w_s_down.astype(BF16),
    }


def _rope_tables(positions):
    inv_freq = ROPE_THETA ** (-jnp.arange(ROPE_HALF, dtype=F32) / ROPE_HALF)
    ang_t = inv_freq[None, :, None] * positions.astype(F32)[:, None, :]
    row = jnp.arange(ROPE_HALF)[:, None]
    lane = jnp.arange(HEAD_PAD)[None, :]
    first = lane == row + QK_NOPE
    second = lane == row + QK_NOPE + ROPE_HALF
    return {
        "cos_t": jnp.cos(ang_t), "sin_t": jnp.sin(ang_t),
        "e_cos": (first | second).astype(BF16),
        "e_sin": (second.astype(F32) - first.astype(F32)).astype(BF16),
    }


def kernel(x, c, positions, w_ada, b_ada, g_mix, g_ffn, w_in, conv_w, g_q_a, w_uq, g_kv_a, w_ukv,
           g_qn, g_kn, w_conv_out, w_attn_out, w_o, w_router, b_router, w_e_gate, w_e_up, w_e_down,
           w_s_gate, w_s_up, w_s_down):
    depth = w_ada.shape[0]
    b = x.shape[0]
    c_pad = jnp.pad(c, ((0, 8 - b), (0, 0)))
    mod = _modulation(c_pad, w_ada, b_ada)[:, :b, :]
    mod = mod.reshape(depth, b, 6, 1, D_MODEL)

    tabs = _rope_tables(positions)
    wr_hi, wr_lo = _split_bf16(w_router.T)
    wr = {"w_hl": jnp.concatenate([wr_hi, wr_lo], axis=0), "b": b_router.reshape(N_EXPERTS, 1)}
    wp = _prep_weights(g_mix, g_ffn, w_in, conv_w, g_q_a, w_uq, g_kv_a, w_ukv, g_qn, g_kn, w_conv_out,
                       w_attn_out, w_o, w_e_gate, w_e_up, w_e_down, w_s_gate, w_s_up, w_s_down)

    for l in range(depth):
        sh1, sc1, gt1, sh2, sc2, gt2 = (mod[l, :, j] for j in range(6))
        cb, u, sga, sgb, qt, k, vt = _proj_call(l, x, sc1, sh1, wp, tabs)
        ot = _attn_call(qt, k, vt)
        x = _merge_call(l, x, cb, u, sga, sgb, ot, gt1, wp)
        x = _moe_call(l, x, sc2, sh2, gt2, wp, wr)
    return x
```

```python
import functools

import jax
import jax.numpy as jnp
from jax import lax
from jax.experimental import pallas as pl
from jax.experimental.pallas import tpu as pltpu

F32 = jnp.float32
BF16 = jnp.bfloat16

D_MODEL = 1024
CONV_WIDTH = 512
N_HEADS = 8
QK_NOPE = 64
QK_ROPE = 32
V_DIM = 64
QK_DIM = QK_NOPE + QK_ROPE
Q_RANK = 256
KV_RANK = 128
ROPE_THETA = 10000.0
N_EXPERTS = 16
N_GROUPS = 4
GROUP_SIZE = N_EXPERTS // N_GROUPS
D_EXPERT = 256
D_SHARED = 256
EPS = 1e-6

HEAD_PAD = 128
V_ROWS = 80
LAT_START = 3 * CONV_WIDTH
ROPE_HALF = QK_ROPE // 2

TOKEN_TILE = 512
ATTN_TILE = 256
ROW_BLOCK = 144
LOG2_E = 1.4426950408889634
VMEM_LIMIT = 56 * 1024 * 1024
NEG_BIG = -1e30


def _const_spec(shape):
    nd = len(shape)
    return pl.BlockSpec(shape, lambda *_: (0,) * nd, pipeline_mode=pl.Buffered(1))


def _layer_spec(shape, layer):
    nd = len(shape)
    return pl.BlockSpec((None,) + tuple(shape), lambda *_: (layer,) + (0,) * nd, pipeline_mode=pl.Buffered(1))


def _silu(v):
    return v * jax.nn.sigmoid(v)


def _split_bf16(v):
    hi = v.astype(BF16)
    return hi, (v - hi.astype(F32)).astype(BF16)


def _mod_kernel(c_ref, w_ref, b_ref, o_ref):
    hi, lo = _split_bf16(_silu(c_ref[...]))
    w = w_ref[...].astype(BF16)
    o_ref[...] = (jnp.dot(hi, w, preferred_element_type=F32) + jnp.dot(lo, w, preferred_element_type=F32)
                  + b_ref[...])


def _modulation(c_pad, w_ada, b_ada):
    depth = w_ada.shape[0]
    nblk = w_ada.shape[2] // D_MODEL
    rows = c_pad.shape[0]
    return pl.pallas_call(
        _mod_kernel,
        out_shape=jax.ShapeDtypeStruct((depth, rows, w_ada.shape[2]), F32),
        grid=(depth, nblk),
        in_specs=[
            pl.BlockSpec((rows, D_MODEL), lambda l, j: (0, 0)),
            pl.BlockSpec((None, D_MODEL, D_MODEL), lambda l, j: (l, 0, j)),
            pl.BlockSpec((None, 1, D_MODEL), lambda l, j: (l, 0, j)),
        ],
        out_specs=pl.BlockSpec((None, rows, D_MODEL), lambda l, j: (l, 0, j)),
        compiler_params=pltpu.CompilerParams(
            dimension_semantics=("arbitrary", "arbitrary"), vmem_limit_bytes=VMEM_LIMIT),
        name="adaln_mod",
    )(c_pad, w_ada, b_ada.reshape(depth, 1, -1))


def _store_token_tiles(ref, row0, val):
    rows = val.shape[0]
    for j in range(val.shape[1] // ATTN_TILE):
        ref[j, row0:row0 + rows, :] = val[:, j * ATTN_TILE:(j + 1) * ATTN_TILE]


def _proj_kernel(x_ref, g_ref, sc_ref, sh_ref, wmain_ref, wpe_ref, wgate_ref, gqa_ref, wuqt_ref, gkva_ref, wuk_ref,
                 wuvt_ref, gq_ref, gk_ref, gksw_ref, ones_ref, ecos_ref, esin_ref, cost_ref, sint_ref,
                 cb_ref, u_ref, sga_ref, sgb_ref, qt_ref, k_ref, vt_ref):
    x = x_ref[...]
    r = lax.rsqrt(jnp.mean(x * x, axis=-1, keepdims=True) + EPS)
    h = (x * r) * (g_ref[...] * (1.0 + sc_ref[...])) + sh_ref[...]
    hb = h.astype(BF16)
    tm = x.shape[0]

    lat = jnp.dot(hb, wmain_ref[:, LAT_START:LAT_START + Q_RANK + KV_RANK], preferred_element_type=F32)
    cq = lat[:, 0:Q_RANK]
    ckv = lat[:, Q_RANK:Q_RANK + KV_RANK]
    pe = jnp.dot(hb, wpe_ref[...], preferred_element_type=F32)
    kpe = pe[:, 0:HEAD_PAD]
    kpe_sw = pe[:, HEAD_PAD:2 * HEAD_PAD]

    cqn = (cq * lax.rsqrt(jnp.mean(cq * cq, axis=-1, keepdims=True) + EPS) * gqa_ref[...]).astype(BF16)
    ckvn = (ckv * lax.rsqrt(jnp.mean(ckv * ckv, axis=-1, keepdims=True) + EPS) * gkva_ref[...]).astype(BF16)

    nt = (((1,), (1,)), ((), ()))
    qt = lax.dot_general(wuqt_ref[...], cqn, nt, preferred_element_type=F32)
    cost = cost_ref[...]
    sint = sint_ref[...]
    gq = gq_ref[...]
    for hh in range(N_HEADS):
        base = hh * HEAD_PAD
        qh = qt[base:base + HEAD_PAD]
        rr = lax.rsqrt(jnp.sum(qh * qh, axis=0, keepdims=True) * (1.0 / QK_DIM) + EPS)
        qn = qh * rr * gq
        x1 = qn[QK_NOPE:QK_NOPE + ROPE_HALF]
        x2 = qn[QK_NOPE + ROPE_HALF:QK_DIM]
        _store_token_tiles(qt_ref, base, qn[0:QK_NOPE].astype(BF16))
        _store_token_tiles(qt_ref, base + QK_NOPE, (x1 * cost - x2 * sint).astype(BF16))
        _store_token_tiles(qt_ref, base + QK_NOPE + ROPE_HALF, (x2 * cost + x1 * sint).astype(BF16))
        _store_token_tiles(qt_ref, base + QK_DIM, jnp.zeros((HEAD_PAD - QK_DIM, tm), BF16))

    kn = jnp.dot(ckvn, wuk_ref[...], preferred_element_type=F32)
    tn = (((0,), (0,)), ((), ()))
    place = lambda tab, e: sum(lax.dot_general(part, e, tn, preferred_element_type=F32) for part in _split_bf16(tab))
    lane = lax.broadcasted_iota(jnp.int32, (1, HEAD_PAD), 1)
    ck = jnp.where(lane < QK_NOPE, 1.0, 0.0) + place(cost, ecos_ref[...])
    ss = place(sint, esin_ref[...])
    direct = gk_ref[...] * ck
    partner = kpe_sw * (gksw_ref[...] * ss)
    ones = ones_ref[...]
    kpe2 = jnp.concatenate([kpe, kpe], axis=1)
    direct2 = jnp.concatenate([direct, direct], axis=1)
    partner2 = jnp.concatenate([partner, partner], axis=1)
    for pair in range(N_HEADS // 2):
        base = pair * 2 * HEAD_PAD
        kh = kn[:, base:base + 2 * HEAD_PAD] + kpe2
        ssq = jnp.dot((kh * kh).astype(BF16), ones, preferred_element_type=F32)
        rr = lax.rsqrt(ssq * (1.0 / QK_DIM) + EPS)
        k_ref[:, base:base + 2 * HEAD_PAD] = (rr * (kh * direct2 + partner2)).astype(BF16)

    vt = lax.dot_general(wuvt_ref[...], ckvn, nt, preferred_element_type=F32)
    for hh in range(N_HEADS):
        _store_token_tiles(vt_ref, hh * V_ROWS, vt[hh * V_DIM:(hh + 1) * V_DIM].astype(BF16))
        _store_token_tiles(vt_ref, hh * V_ROWS + V_DIM, jnp.ones((V_ROWS - V_DIM, tm), BF16))

    pc = jnp.dot(hb, wmain_ref[:, 0:LAT_START], preferred_element_type=F32)
    cb_ref[...] = pc[:, 0:CONV_WIDTH].astype(BF16)
    u_ref[...] = (pc[:, CONV_WIDTH:2 * CONV_WIDTH] * pc[:, 2 * CONV_WIDTH:3 * CONV_WIDTH]).astype(BF16)

    pg = jnp.dot(hb, wgate_ref[...], preferred_element_type=F32)
    sga_ref[...] = jax.nn.sigmoid(pg[:, 0:D_MODEL]).astype(BF16)
    sgb_ref[...] = jax.nn.sigmoid(pg[:, D_MODEL:2 * D_MODEL]).astype(BF16)


def _proj_call(layer, x, sc1, sh1, wp, tabs):
    b, s, _ = x.shape
    tm = TOKEN_TILE
    at = ATTN_TILE
    tile = lambda w: pl.BlockSpec((None, tm, w), lambda bi, i: (bi, i, 0))
    tile_t = lambda rws: pl.BlockSpec((None, rws, tm), lambda bi, i: (bi, 0, i))
    per_batch = pl.BlockSpec((None, 1, D_MODEL), lambda bi, i: (bi, 0, 0))
    lw = lambda *shape: _layer_spec(shape, layer)
    in_specs = [
        tile(D_MODEL), lw(1, D_MODEL), per_batch, per_batch,
        lw(D_MODEL, LAT_START + Q_RANK + KV_RANK), lw(D_MODEL, 2 * HEAD_PAD), lw(D_MODEL, 2 * D_MODEL), lw(1, Q_RANK),
        lw(N_HEADS * HEAD_PAD, Q_RANK), lw(1, KV_RANK),
        lw(KV_RANK, N_HEADS * HEAD_PAD), lw(N_HEADS * V_DIM, KV_RANK),
        lw(HEAD_PAD, 1), lw(1, HEAD_PAD), lw(1, HEAD_PAD),
        _const_spec((2 * HEAD_PAD, 2 * HEAD_PAD)), _const_spec((ROPE_HALF, HEAD_PAD)), _const_spec((ROPE_HALF, HEAD_PAD)),
        tile_t(ROPE_HALF), tile_t(ROPE_HALF),
    ]
    out_shape = [
        jax.ShapeDtypeStruct((b, s, CONV_WIDTH), BF16),
        jax.ShapeDtypeStruct((b, s, CONV_WIDTH), BF16),
        jax.ShapeDtypeStruct((b, s, D_MODEL), BF16),
        jax.ShapeDtypeStruct((b, s, D_MODEL), BF16),
        jax.ShapeDtypeStruct((b, s // at, N_HEADS * HEAD_PAD, at), BF16),
        jax.ShapeDtypeStruct((b, s, N_HEADS * HEAD_PAD), BF16),
        jax.ShapeDtypeStruct((b, s // at, N_HEADS * V_ROWS, at), BF16),
    ]
    att_tiles = lambda rws: pl.BlockSpec((None, tm // at, rws, at), lambda bi, i: (bi, i, 0, 0))
    out_specs = [tile(CONV_WIDTH), tile(CONV_WIDTH), tile(D_MODEL), tile(D_MODEL),
                 att_tiles(N_HEADS * HEAD_PAD), tile(N_HEADS * HEAD_PAD), att_tiles(N_HEADS * V_ROWS)]
    return pl.pallas_call(
        _proj_kernel, out_shape=out_shape, grid=(b, s // tm),
        in_specs=in_specs, out_specs=out_specs,
        compiler_params=pltpu.CompilerParams(
            dimension_semantics=("parallel", "parallel"), vmem_limit_bytes=VMEM_LIMIT),
        name="norm_in_proj",
    )(x, wp["g_mix"], sc1, sh1, wp["w_in_main"], wp["w_in_pe"], wp["w_in_gate"], wp["g_q_a"], wp["w_uq_t"], wp["g_kv_a"], wp["w_uk"],
      wp["w_uv_t"], wp["g_q"], wp["g_k"], wp["g_k_sw"],
      jnp.kron(jnp.eye(2, dtype=BF16), jnp.ones((HEAD_PAD, HEAD_PAD), BF16)),
      tabs["e_cos"], tabs["e_sin"], tabs["cos_t"], tabs["sin_t"])


def _attn_kernel(qt_ref, k_ref, vt_ref, ot_ref, s0, s1, p0, p1, cm0, cm1, al0, al1):
    s_bufs = (s0, s1)
    p_bufs = (p0, p1)
    cmax_bufs = (cm0, cm1)
    alpha_bufs = (al0, al1)
    nq, _, tq = qt_ref.shape
    nk, _, tk = vt_ref.shape

    def score_chunk(qt, c, dst):
        sc = jnp.dot(k_ref[c * tk:(c + 1) * tk, :], qt, preferred_element_type=F32)
        dst[c] = sc
        return jnp.max(sc, axis=0, keepdims=True)

    def step(i, cur, with_scores=True):
        other = 1 - cur
        qt_next = qt_ref[i + 1] if with_scores else None
        m = jnp.full((1, tq), NEG_BIG, F32)
        acc = jnp.zeros((V_ROWS, tq), F32)
        for c in range(nk):
            m_new = jnp.maximum(m, cmax_bufs[cur][c:c + 1, :])
            alpha_bufs[cur][c:c + 1, :] = jnp.exp2(m - m_new)
            p_bufs[cur][c] = jnp.exp2(s_bufs[cur][c] - m_new).astype(BF16)
            m = m_new
            if with_scores:
                cmax_bufs[other][c:c + 1, :] = score_chunk(qt_next, c, s_bufs[other])
            acc = alpha_bufs[other][c:c + 1, :] * acc + jnp.dot(
                vt_ref[c], p_bufs[other][c], preferred_element_type=F32)
        ot_ref[jnp.maximum(i - 1, 0)] = (acc[0:V_DIM] / acc[V_DIM:V_DIM + 1]).astype(BF16)

    def trip(t, carry):
        step(2 * t, 0)
        step(2 * t + 1, 1)
        return carry

    qt0 = qt_ref[0]
    for c in range(nk):
        cm0[c:c + 1, :] = score_chunk(qt0, c, s0)
        p1[c] = jnp.ones((tk, tq), BF16)
    al1[...] = jnp.ones((nk, tq), F32)
    lax.fori_loop(0, nq // 2 - 1, trip, 0)
    step(nq - 2, 0)
    step(nq - 1, 1, with_scores=False)
    acc = jnp.zeros((V_ROWS, tq), F32)
    for c in range(nk):
        acc = al1[c:c + 1, :] * acc + jnp.dot(vt_ref[c], p1[c], preferred_element_type=F32)
    ot_ref[nq - 1] = (acc[0:V_DIM] / acc[V_DIM:V_DIM + 1]).astype(BF16)


def _attn_call(qt, k, vt):
    b, nq, _, tq = qt.shape
    _, nk, _, tk = vt.shape
    s = k.shape[1]
    assert nq % 2 == 0 and nq >= 4 and nk * tk == s
    return pl.pallas_call(
        _attn_kernel,
        out_shape=jax.ShapeDtypeStruct((b, nq, N_HEADS * V_DIM, tq), BF16),
        grid=(b, N_HEADS),
        in_specs=[
            pl.BlockSpec((None, nq, HEAD_PAD, tq), lambda bi, hh: (bi, 0, hh, 0)),
            pl.BlockSpec((None, s, HEAD_PAD), lambda bi, hh: (bi, 0, hh)),
            pl.BlockSpec((None, nk, V_ROWS, tk), lambda bi, hh: (bi, 0, hh, 0)),
        ],
        out_specs=pl.BlockSpec((None, nq, V_DIM, tq), lambda bi, hh: (bi, 0, hh, 0)),
        scratch_shapes=([pltpu.VMEM((nk, tk, tq), F32)] * 2 + [pltpu.VMEM((nk, tk, tq), BF16)] * 2
                        + [pltpu.VMEM((nk, tq), F32)] * 4),
        compiler_params=pltpu.CompilerParams(
            dimension_semantics=("parallel", "parallel"), vmem_limit_bytes=VMEM_LIMIT),
        name="attention_t",
    )(qt, k, vt)


def _merge_kernel(x_ref, cb_ref, u_ref, uprev_ref, unext_ref, sga_ref, sgb_ref, ot_ref,
                  cw_ref, wco_ref, wao_ref, wo_ref, gt_ref, xo_ref):
    i = pl.program_id(1)
    last = pl.num_programs(1) - 1
    u = u_ref[...].astype(F32)
    tm = u.shape[0]
    halo = uprev_ref.shape[0]
    prev = jnp.where(i > 0, uprev_ref[halo - 1:halo, :].astype(F32), 0.0)
    nxt = jnp.where(i < last, unext_ref[0:1, :].astype(F32), 0.0)
    row = lax.broadcasted_iota(jnp.int32, u.shape, 0)
    u_m = jnp.where(row == 0, prev, pltpu.roll(u, 1, 0))
    u_p = jnp.where(row == tm - 1, nxt, pltpu.roll(u, tm - 1, 0))
    cw = cw_ref[...]
    conv = cw[0:1, :] * u_m + cw[1:2, :] * u + cw[2:3, :] * u_p
    za = (cb_ref[...].astype(F32) * conv).astype(BF16)
    ya = jnp.dot(za, wco_ref[...], preferred_element_type=F32)
    tn = (((0,), (0,)), ((), ()))
    yb = jnp.concatenate(
        [lax.dot_general(ot_ref[j], wao_ref[...], tn, preferred_element_type=F32) for j in range(ot_ref.shape[0])],
        axis=0)
    merged = (sga_ref[...].astype(F32) * ya + sgb_ref[...].astype(F32) * yb).astype(BF16)
    y = jnp.dot(merged, wo_ref[...], preferred_element_type=F32)
    xo_ref[...] = x_ref[...] + gt_ref[...] * y


def _merge_call(layer, x, cb, u, sga, sgb, ot, gt1, wp):
    b, s, _ = x.shape
    lw = lambda *shape: _layer_spec(shape, layer)
    tm = TOKEN_TILE
    halo = 16
    nh = tm // halo
    tile = lambda w: pl.BlockSpec((None, tm, w), lambda bi, i: (bi, i, 0))
    per_batch = pl.BlockSpec((None, 1, D_MODEL), lambda bi, i: (bi, 0, 0))
    in_specs = [
        tile(D_MODEL), tile(CONV_WIDTH), tile(CONV_WIDTH),
        pl.BlockSpec((None, halo, CONV_WIDTH), lambda bi, i: (bi, jnp.maximum(i * nh - 1, 0), 0)),
        pl.BlockSpec((None, halo, CONV_WIDTH), lambda bi, i: (bi, jnp.minimum((i + 1) * nh, s // halo - 1), 0)),
        tile(D_MODEL), tile(D_MODEL),
        pl.BlockSpec((None, tm // ATTN_TILE, N_HEADS * V_DIM, ATTN_TILE), lambda bi, i: (bi, i, 0, 0)),
        lw(3, CONV_WIDTH), lw(CONV_WIDTH, D_MODEL), lw(N_HEADS * V_DIM, D_MODEL), lw(D_MODEL, D_MODEL), per_batch,
    ]
    return pl.pallas_call(
        _merge_kernel, out_shape=jax.ShapeDtypeStruct(x.shape, F32), grid=(b, s // tm),
        in_specs=in_specs, out_specs=tile(D_MODEL),
        compiler_params=pltpu.CompilerParams(
            dimension_semantics=("parallel", "parallel"), vmem_limit_bytes=VMEM_LIMIT),
        name="merge_out_proj",
    )(x, cb, u, u, u, sga, sgb, ot, wp["conv_w"], wp["w_conv_out"], wp["w_attn_out"], wp["w_o"], gt1)


def _route(logits_t, bias_col):
    scores = jax.nn.sigmoid(logits_t)
    biased = scores + bias_col
    neg_inf = jnp.float32(-jnp.inf)
    gscores = []
    for g in range(N_GROUPS):
        rws = [biased[g * GROUP_SIZE + j:g * GROUP_SIZE + j + 1] for j in range(GROUP_SIZE)]
        best = None
        for a in range(GROUP_SIZE):
            for bb in range(a + 1, GROUP_SIZE):
                pair = rws[a] + rws[bb]
                best = pair if best is None else jnp.maximum(best, pair)
        gscores.append(best)
    top = gscores[0]
    sel = jnp.zeros_like(top, dtype=jnp.int32)
    for g in range(1, N_GROUPS):
        better = gscores[g] > top
        sel = jnp.where(better, g, sel)
        top = jnp.where(better, gscores[g], top)
    eidx = lax.broadcasted_iota(jnp.int32, biased.shape, 0)
    masked = jnp.where(eidx // GROUP_SIZE == sel, biased, neg_inf)
    m1 = jnp.max(masked, axis=0, keepdims=True)
    i1 = jnp.min(jnp.where(masked == m1, eidx, N_EXPERTS), axis=0, keepdims=True)
    pick1 = eidx == i1
    masked2 = jnp.where(pick1, neg_inf, masked)
    m2 = jnp.max(masked2, axis=0, keepdims=True)
    i2 = jnp.min(jnp.where(masked2 == m2, eidx, N_EXPERTS), axis=0, keepdims=True)
    pick2 = eidx == i2
    s1 = jnp.sum(jnp.where(pick1, scores, 0.0), axis=0, keepdims=True)
    s2 = jnp.sum(jnp.where(pick2, scores, 0.0), axis=0, keepdims=True)
    return jnp.where(pick1 | pick2, scores / (s1 + s2), 0.0), sel


def _moe_kernel(x_ref, g_ref, sc_ref, sh_ref, gt_ref, wr_ref, br_ref, tri_ref,
                wg_ref, wu_ref, wdn_ref, wsg_ref, wsu_ref, wsdn_ref, xo_ref, hs_ref, cs_ref, ys_ref):
    x = x_ref[...]
    tm = x.shape[0]
    tpad = hs_ref.shape[0]
    r = lax.rsqrt(jnp.mean(x * x, axis=-1, keepdims=True) + EPS)
    h = (x * r) * (g_ref[...] * (1.0 + sc_ref[...])) + sh_ref[...]
    hi, lo = _split_bf16(h)

    nt = (((1,), (1,)), ((), ()))
    l_hi = lax.dot_general(wr_ref[...], hi, nt, preferred_element_type=F32)
    l_lo = lax.dot_general(wr_ref[0:N_EXPERTS, :], lo, nt, preferred_element_type=F32)
    logits_t = l_hi[0:N_EXPERTS] + l_hi[N_EXPERTS:2 * N_EXPERTS] + l_lo
    comb_t, sel = _route(logits_t, br_ref[...])
    comb = jnp.concatenate([comb_t, jnp.zeros((128 - N_EXPERTS, tm), F32)], axis=0).T

    member = (lax.broadcasted_iota(jnp.int32, (8, tm), 0) == sel).astype(F32)
    rank = jnp.dot(member.astype(BF16), tri_ref[...], preferred_element_type=F32)
    count = jnp.sum(member, axis=1, keepdims=True)
    nblk = jnp.floor((count + (ROW_BLOCK - 1)) * (1.0 / ROW_BLOCK))
    start = []
    rows_so_far = jnp.zeros((1, 1), F32)
    for g in range(N_GROUPS):
        start.append(rows_so_far)
        rows_so_far = rows_so_far + nblk[g:g + 1] * ROW_BLOCK
    dest = jnp.zeros((1, tm), F32)
    for g in range(N_GROUPS):
        dest = dest + member[g:g + 1] * (start[g] + rank[g:g + 1])
    perm = (lax.broadcasted_iota(jnp.int32, (tpad, tm), 0) == dest.astype(jnp.int32)).astype(BF16)

    hs_ref[...] = jnp.dot(perm, hi, preferred_element_type=F32).astype(BF16)
    c1 = comb.astype(BF16)
    rest = comb - c1.astype(F32)
    c2, c3 = _split_bf16(rest)
    packed = jnp.concatenate([c1[:, 0:N_EXPERTS], c2[:, 0:N_EXPERTS], c3[:, 0:N_EXPERTS],
                              jnp.zeros((tm, 128 - 3 * N_EXPERTS), BF16)], axis=1)
    cs_ref[...] = jnp.dot(perm, packed, preferred_element_type=F32)
    ys_ref[...] = jnp.zeros(ys_ref.shape, BF16)

    for g in range(N_GROUPS):
        row0 = jnp.sum(start[g]).astype(jnp.int32)
        nb = jnp.sum(nblk[g:g + 1]).astype(jnp.int32)

        def block(blk, carry, g=g, row0=row0):
            rows = pl.ds(pl.multiple_of(row0 + blk * ROW_BLOCK, ROW_BLOCK), ROW_BLOCK)
            hb = hs_ref[rows, :]
            cw = cs_ref[rows, :]
            y = jnp.zeros((ROW_BLOCK, D_MODEL), F32)
            for j in range(GROUP_SIZE):
                e = g * GROUP_SIZE + j
                gate = jnp.dot(hb, wg_ref[e], preferred_element_type=F32)
                up = jnp.dot(hb, wu_ref[e], preferred_element_type=F32)
                w = (cw[:, e:e + 1] + cw[:, N_EXPERTS + e:N_EXPERTS + e + 1]
                     + cw[:, 2 * N_EXPERTS + e:2 * N_EXPERTS + e + 1])
                y = y + jnp.dot((_silu(gate) * up * w).astype(BF16), wdn_ref[e * D_EXPERT:(e + 1) * D_EXPERT, :],
                                preferred_element_type=F32)
            ys_ref[rows, :] = y.astype(BF16)
            return carry

        lax.fori_loop(0, nb, block, 0)

    tn = (((0,), (0,)), ((), ()))
    routed = lax.dot_general(perm, ys_ref[...], tn, preferred_element_type=F32)
    a_s = _silu(jnp.dot(hi, wsg_ref[...], preferred_element_type=F32)) * jnp.dot(
        hi, wsu_ref[...], preferred_element_type=F32)
    y = routed + jnp.dot(a_s.astype(BF16), wsdn_ref[...], preferred_element_type=F32)
    xo_ref[...] = x + gt_ref[...] * y


def _moe_call(layer, x, sc2, sh2, gt2, wp, wr):
    b, s, _ = x.shape
    tm = TOKEN_TILE
    tile = pl.BlockSpec((None, tm, D_MODEL), lambda bi, i: (bi, i, 0))
    per_batch = pl.BlockSpec((None, 1, D_MODEL), lambda bi, i: (bi, 0, 0))
    lw = lambda *shape: _layer_spec(shape, layer)
    tpad = tm + N_GROUPS * ROW_BLOCK
    before = lax.broadcasted_iota(jnp.int32, (tm, tm), 0) < lax.broadcasted_iota(jnp.int32, (tm, tm), 1)
    in_specs = [
        tile, lw(1, D_MODEL), per_batch, per_batch, per_batch,
        _const_spec((2 * N_EXPERTS, D_MODEL)), _const_spec((N_EXPERTS, 1)), _const_spec((tm, tm)),
        lw(N_EXPERTS, D_MODEL, D_EXPERT), lw(N_EXPERTS, D_MODEL, D_EXPERT), lw(N_EXPERTS * D_EXPERT, D_MODEL),
        lw(D_MODEL, D_SHARED), lw(D_MODEL, D_SHARED), lw(D_SHARED, D_MODEL),
    ]
    return pl.pallas_call(
        _moe_kernel, out_shape=jax.ShapeDtypeStruct(x.shape, F32), grid=(b, s // tm),
        in_specs=in_specs, out_specs=tile,
        scratch_shapes=[pltpu.VMEM((tpad, D_MODEL), BF16), pltpu.VMEM((tpad, 128), F32),
                        pltpu.VMEM((tpad, D_MODEL), BF16)],
        compiler_params=pltpu.CompilerParams(
            dimension_semantics=("parallel", "parallel"), vmem_limit_bytes=VMEM_LIMIT),
        name="moe_ffn",
    )(x, wp["g_ffn"], sc2, sh2, gt2, wr["w_hl"], wr["b"], before.astype(BF16), wp["w_e_gate"], wp["w_e_up"],
      wp["w_e_down"], wp["w_s_gate"], wp["w_s_up"], wp["w_s_down"])


def _prep_weights(g_mix, g_ffn, w_in, conv_w, g_q_a, w_uq, g_kv_a, w_ukv, g_qn, g_kn, w_conv_out, w_attn_out,
                  w_o, w_e_gate, w_e_up, w_e_down, w_s_gate, w_s_up, w_s_down):
    depth = w_in.shape[0]
    split = 3 * CONV_WIDTH + Q_RANK + KV_RANK
    w_pe = w_in[:, :, split:split + QK_ROPE]
    w_pe_sw = jnp.concatenate([w_pe[:, :, ROPE_HALF:], w_pe[:, :, :ROPE_HALF]], axis=2)
    lane_pad = lambda w: jnp.pad(w, ((0, 0), (0, 0), (QK_NOPE, HEAD_PAD - QK_DIM)))
    w_in_pe = jnp.concatenate([lane_pad(w_pe), lane_pad(w_pe_sw)], axis=2).astype(BF16)

    wq = w_uq.reshape(depth, Q_RANK, N_HEADS, QK_DIM)
    wq = jnp.pad(wq, ((0, 0), (0, 0), (0, 0), (0, HEAD_PAD - QK_DIM))).reshape(depth, Q_RANK, N_HEADS * HEAD_PAD)
    wkv = w_ukv.reshape(depth, KV_RANK, N_HEADS, QK_NOPE + V_DIM)
    wk = jnp.pad(wkv[..., :QK_NOPE], ((0, 0), (0, 0), (0, 0), (0, HEAD_PAD - QK_NOPE)))
    wk = wk.reshape(depth, KV_RANK, N_HEADS * HEAD_PAD)
    wv = wkv[..., QK_NOPE:].reshape(depth, KV_RANK, N_HEADS * V_DIM)
    scale = QK_DIM ** -0.5 * LOG2_E
    g_q = jnp.pad(g_qn * scale, ((0, 0), (0, HEAD_PAD - QK_DIM))).reshape(depth, HEAD_PAD, 1)
    g_k = jnp.pad(g_kn, ((0, 0), (0, HEAD_PAD - QK_DIM))).reshape(depth, 1, HEAD_PAD)
    g_rope = g_kn[:, QK_NOPE:]
    g_k_sw = jnp.pad(jnp.concatenate([g_rope[:, ROPE_HALF:], g_rope[:, :ROPE_HALF]], axis=1),
                     ((0, 0), (QK_NOPE, HEAD_PAD - QK_DIM))).reshape(depth, 1, HEAD_PAD)
    return {
        "g_mix": g_mix.reshape(depth, 1, D_MODEL),
        "g_ffn": g_ffn.reshape(depth, 1, D_MODEL),
        "w_in_main": w_in[:, :, :split].astype(BF16),
        "w_in_pe": w_in_pe,
        "w_in_gate": w_in[:, :, split + QK_ROPE:].astype(BF16),
        "g_q_a": g_q_a.reshape(depth, 1, Q_RANK),
        "w_uq_t": wq.transpose(0, 2, 1).astype(BF16),
        "g_kv_a": g_kv_a.reshape(depth, 1, KV_RANK),
        "w_uk": wk.astype(BF16),
        "w_uv_t": wv.transpose(0, 2, 1).astype(BF16),
        "g_q": g_q, "g_k": g_k, "g_k_sw": g_k_sw,
        "conv_w": conv_w,
        "w_conv_out": w_conv_out.astype(BF16),
        "w_attn_out": w_attn_out.astype(BF16),
        "w_o": w_o.astype(BF16),
        "w_e_gate": w_e_gate.astype(BF16),
        "w_e_up": w_e_up.astype(BF16),
        "w_e_down": w_e_down.astype(BF16).reshape(depth, N_EXPERTS * D_EXPERT, D_MODEL),
        "w_s_gate": w_s_gate.astype(BF16),
        "w_s_up": w_s_up.astype(BF16),
        "w_s_down": w_s_down.astype(BF16),
    }


def _rope_tables(positions):
    inv_freq = ROPE_THETA ** (-jnp.arange(ROPE_HALF, dtype=F32) / ROPE_HALF)
    ang_t = inv_freq[None, :, None] * positions.astype(F32)[:, None, :]
    row = jnp.arange(ROPE_HALF)[:, None]
    lane = jnp.arange(HEAD_PAD)[None, :]
    first = lane == row + QK_NOPE
    second = lane == row + QK_NOPE + ROPE_HALF
    return {
        "cos_t": jnp.cos(ang_t), "sin_t": jnp.sin(ang_t),
        "e_cos": (first | second).astype(BF16),
        "e_sin": (second.astype(F32) - first.astype(F32)).astype(BF16),
    }


def kernel(x, c, positions, w_ada, b_ada, g_mix, g_ffn, w_in, conv_w, g_q_a, w_uq, g_kv_a, w_ukv,
           g_qn, g_kn, w_conv_out, w_attn_out, w_o, w_router, b_router, w_e_gate, w_e_up, w_e_down,
           w_s_gate, w_s_up, w_s_down):
    depth = w_ada.shape[0]
    b = x.shape[0]
    c_pad = jnp.pad(c, ((0, 8 - b), (0, 0)))
    mod = _modulation(c_pad, w_ada, b_ada)[:, :b, :]
    mod = mod.reshape(depth, b, 6, 1, D_MODEL)

    tabs = _rope_tables(positions)
    wr_hi, wr_lo = _split_bf16(w_router.T)
    wr = {"w_hl": jnp.concatenate([wr_hi, wr_lo], axis=0), "b": b_router.reshape(N_EXPERTS, 1)}
    wp = _prep_weights(g_mix, g_ffn, w_in, conv_w, g_q_a, w_uq, g_kv_a, w_ukv, g_qn, g_kn, w_conv_out,
                       w_attn_out, w_o, w_e_gate, w_e_up, w_e_down, w_s_gate, w_s_up, w_s_down)

    for l in range(depth):
        sh1, sc1, gt1, sh2, sc2, gt2 = (mod[l, :, j] for j in range(6))
        cb, u, sga, sgb, qt, k, vt = _proj_call(l, x, sc1, sh1, wp, tabs)
        ot = _attn_call(qt, k, vt)
        x = _merge_call(l, x, cb, u, sga, sgb, ot, gt1, wp)
        x = _moe_call(l, x, sc2, sh2, gt2, wp, wr)
    return x
```

```python
import functools

import jax
import jax.numpy as jnp
from jax import lax
from jax.experimental import pallas as pl
from jax.experimental.pallas import tpu as pltpu

F32 = jnp.float32
BF16 = jnp.bfloat16

D_MODEL = 1024
CONV_WIDTH = 512
N_HEADS = 8
QK_NOPE = 64
QK_ROPE = 32
V_DIM = 64
QK_DIM = QK_NOPE + QK_ROPE
Q_RANK = 256
KV_RANK = 128
ROPE_THETA = 10000.0
N_EXPERTS = 16
N_GROUPS = 4
GROUP_SIZE = N_EXPERTS // N_GROUPS
D_EXPERT = 256
D_SHARED = 256
EPS = 1e-6

HEAD_PAD = 128
V_ROWS = 80
LAT_START = 3 * CONV_WIDTH
ROPE_HALF = QK_ROPE // 2

TOKEN_TILE = 512
ATTN_TILE = 256
ROW_BLOCK = 160
LOG2_E = 1.4426950408889634
VMEM_LIMIT = 56 * 1024 * 1024
NEG_BIG = -1e30


def _const_spec(shape):
    nd = len(shape)
    return pl.BlockSpec(shape, lambda *_: (0,) * nd, pipeline_mode=pl.Buffered(1))


def _layer_spec(shape, layer):
    nd = len(shape)
    return pl.BlockSpec((None,) + tuple(shape), lambda *_: (layer,) + (0,) * nd, pipeline_mode=pl.Buffered(1))


def _silu(v):
    return v * jax.nn.sigmoid(v)


def _split_bf16(v):
    hi = v.astype(BF16)
    return hi, (v - hi.astype(F32)).astype(BF16)


def _mod_kernel(c_ref, w_ref, b_ref, o_ref):
    hi, lo = _split_bf16(_silu(c_ref[...]))
    w = w_ref[...].astype(BF16)
    o_ref[...] = (jnp.dot(hi, w, preferred_element_type=F32) + jnp.dot(lo, w, preferred_element_type=F32)
                  + b_ref[...])


def _modulation(c_pad, w_ada, b_ada):
    depth = w_ada.shape[0]
    nblk = w_ada.shape[2] // D_MODEL
    rows = c_pad.shape[0]
    return pl.pallas_call(
        _mod_kernel,
        out_shape=jax.ShapeDtypeStruct((depth, rows, w_ada.shape[2]), F32),
        grid=(depth, nblk),
        in_specs=[
            pl.BlockSpec((rows, D_MODEL), lambda l, j: (0, 0)),
            pl.BlockSpec((None, D_MODEL, D_MODEL), lambda l, j: (l, 0, j)),
            pl.BlockSpec((None, 1, D_MODEL), lambda l, j: (l, 0, j)),
        ],
        out_specs=pl.BlockSpec((None, rows, D_MODEL), lambda l, j: (l, 0, j)),
        compiler_params=pltpu.CompilerParams(
            dimension_semantics=("arbitrary", "arbitrary"), vmem_limit_bytes=VMEM_LIMIT),
        name="adaln_mod",
    )(c_pad, w_ada, b_ada.reshape(depth, 1, -1))


def _store_token_tiles(ref, row0, val):
    rows = val.shape[0]
    for j in range(val.shape[1] // ATTN_TILE):
        ref[j, row0:row0 + rows, :] = val[:, j * ATTN_TILE:(j + 1) * ATTN_TILE]


def _proj_kernel(x_ref, g_ref, sc_ref, sh_ref, wmain_ref, wpe_ref, wgate_ref, gqa_ref, wuqt_ref, gkva_ref, wuk_ref,
                 wuvt_ref, gq_ref, gk_ref, gksw_ref, ones_ref, ecos_ref, esin_ref, cost_ref, sint_ref,
                 cb_ref, u_ref, sga_ref, sgb_ref, qt_ref, k_ref, vt_ref):
    x = x_ref[...]
    r = lax.rsqrt(jnp.mean(x * x, axis=-1, keepdims=True) + EPS)
    h = (x * r) * (g_ref[...] * (1.0 + sc_ref[...])) + sh_ref[...]
    hb = h.astype(BF16)
    tm = x.shape[0]

    lat = jnp.dot(hb, wmain_ref[:, LAT_START:LAT_START + Q_RANK + KV_RANK], preferred_element_type=F32)
    cq = lat[:, 0:Q_RANK]
    ckv = lat[:, Q_RANK:Q_RANK + KV_RANK]
    pe = jnp.dot(hb, wpe_ref[...], preferred_element_type=F32)
    kpe = pe[:, 0:HEAD_PAD]
    kpe_sw = pe[:, HEAD_PAD:2 * HEAD_PAD]

    cqn = (cq * lax.rsqrt(jnp.mean(cq * cq, axis=-1, keepdims=True) + EPS) * gqa_ref[...]).astype(BF16)
    ckvn = (ckv * lax.rsqrt(jnp.mean(ckv * ckv, axis=-1, keepdims=True) + EPS) * gkva_ref[...]).astype(BF16)

    nt = (((1,), (1,)), ((), ()))
    qt = lax.dot_general(wuqt_ref[...], cqn, nt, preferred_element_type=F32)
    cost = cost_ref[...]
    sint = sint_ref[...]
    gq = gq_ref[...]
    for hh in range(N_HEADS):
        base = hh * HEAD_PAD
        qh = qt[base:base + HEAD_PAD]
        rr = lax.rsqrt(jnp.sum(qh * qh, axis=0, keepdims=True) * (1.0 / QK_DIM) + EPS)
        qn = qh * rr * gq
        x1 = qn[QK_NOPE:QK_NOPE + ROPE_HALF]
        x2 = qn[QK_NOPE + ROPE_HALF:QK_DIM]
        _store_token_tiles(qt_ref, base, qn[0:QK_NOPE].astype(BF16))
        _store_token_tiles(qt_ref, base + QK_NOPE, (x1 * cost - x2 * sint).astype(BF16))
        _store_token_tiles(qt_ref, base + QK_NOPE + ROPE_HALF, (x2 * cost + x1 * sint).astype(BF16))
        _store_token_tiles(qt_ref, base + QK_DIM, jnp.zeros((HEAD_PAD - QK_DIM, tm), BF16))

    kn = jnp.dot(ckvn, wuk_ref[...], preferred_element_type=F32)
    tn = (((0,), (0,)), ((), ()))
    place = lambda tab, e: sum(lax.dot_general(part, e, tn, preferred_element_type=F32) for part in _split_bf16(tab))
    lane = lax.broadcasted_iota(jnp.int32, (1, HEAD_PAD), 1)
    ck = jnp.where(lane < QK_NOPE, 1.0, 0.0) + place(cost, ecos_ref[...])
    ss = place(sint, esin_ref[...])
    direct = gk_ref[...] * ck
    partner = kpe_sw * (gksw_ref[...] * ss)
    ones = ones_ref[...]
    kpe2 = jnp.concatenate([kpe, kpe], axis=1)
    direct2 = jnp.concatenate([direct, direct], axis=1)
    partner2 = jnp.concatenate([partner, partner], axis=1)
    for pair in range(N_HEADS // 2):
        base = pair * 2 * HEAD_PAD
        kh = kn[:, base:base + 2 * HEAD_PAD] + kpe2
        ssq = jnp.dot((kh * kh).astype(BF16), ones, preferred_element_type=F32)
        rr = lax.rsqrt(ssq * (1.0 / QK_DIM) + EPS)
        k_ref[:, base:base + 2 * HEAD_PAD] = (rr * (kh * direct2 + partner2)).astype(BF16)

    vt = lax.dot_general(wuvt_ref[...], ckvn, nt, preferred_element_type=F32)
    for hh in range(N_HEADS):
        _store_token_tiles(vt_ref, hh * V_ROWS, vt[hh * V_DIM:(hh + 1) * V_DIM].astype(BF16))
        _store_token_tiles(vt_ref, hh * V_ROWS + V_DIM, jnp.ones((V_ROWS - V_DIM, tm), BF16))

    pc = jnp.dot(hb, wmain_ref[:, 0:LAT_START], preferred_element_type=F32)
    cb_ref[...] = pc[:, 0:CONV_WIDTH].astype(BF16)
    u_ref[...] = (pc[:, CONV_WIDTH:2 * CONV_WIDTH] * pc[:, 2 * CONV_WIDTH:3 * CONV_WIDTH]).astype(BF16)

    pg = jnp.dot(hb, wgate_ref[...], preferred_element_type=F32)
    sga_ref[...] = jax.nn.sigmoid(pg[:, 0:D_MODEL]).astype(BF16)
    sgb_ref[...] = jax.nn.sigmoid(pg[:, D_MODEL:2 * D_MODEL]).astype(BF16)


def _proj_call(layer, x, sc1, sh1, wp, tabs):
    b, s, _ = x.shape
    tm = TOKEN_TILE
    at = ATTN_TILE
    tile = lambda w: pl.BlockSpec((None, tm, w), lambda bi, i: (bi, i, 0))
    tile_t = lambda rws: pl.BlockSpec((None, rws, tm), lambda bi, i: (bi, 0, i))
    per_batch = pl.BlockSpec((None, 1, D_MODEL), lambda bi, i: (bi, 0, 0))
    lw = lambda *shape: _layer_spec(shape, layer)
    in_specs = [
        tile(D_MODEL), lw(1, D_MODEL), per_batch, per_batch,
        lw(D_MODEL, LAT_START + Q_RANK + KV_RANK), lw(D_MODEL, 2 * HEAD_PAD), lw(D_MODEL, 2 * D_MODEL), lw(1, Q_RANK),
        lw(N_HEADS * HEAD_PAD, Q_RANK), lw(1, KV_RANK),
        lw(KV_RANK, N_HEADS * HEAD_PAD), lw(N_HEADS * V_DIM, KV_RANK),
        lw(HEAD_PAD, 1), lw(1, HEAD_PAD), lw(1, HEAD_PAD),
        _const_spec((2 * HEAD_PAD, 2 * HEAD_PAD)), _const_spec((ROPE_HALF, HEAD_PAD)), _const_spec((ROPE_HALF, HEAD_PAD)),
        tile_t(ROPE_HALF), tile_t(ROPE_HALF),
    ]
    out_shape = [
        jax.ShapeDtypeStruct((b, s, CONV_WIDTH), BF16),
        jax.ShapeDtypeStruct((b, s, CONV_WIDTH), BF16),
        jax.ShapeDtypeStruct((b, s, D_MODEL), BF16),
        jax.ShapeDtypeStruct((b, s, D_MODEL), BF16),
        jax.ShapeDtypeStruct((b, s // at, N_HEADS * HEAD_PAD, at), BF16),
        jax.ShapeDtypeStruct((b, s, N_HEADS * HEAD_PAD), BF16),
        jax.ShapeDtypeStruct((b, s // at, N_HEADS * V_ROWS, at), BF16),
    ]
    att_tiles = lambda rws: pl.BlockSpec((None, tm // at, rws, at), lambda bi, i: (bi, i, 0, 0))
    out_specs = [tile(CONV_WIDTH), tile(CONV_WIDTH), tile(D_MODEL), tile(D_MODEL),
                 att_tiles(N_HEADS * HEAD_PAD), tile(N_HEADS * HEAD_PAD), att_tiles(N_HEADS * V_ROWS)]
    return pl.pallas_call(
        _proj_kernel, out_shape=out_shape, grid=(b, s // tm),
        in_specs=in_specs, out_specs=out_specs,
        compiler_params=pltpu.CompilerParams(
            dimension_semantics=("parallel", "parallel"), vmem_limit_bytes=VMEM_LIMIT),
        name="norm_in_proj",
    )(x, wp["g_mix"], sc1, sh1, wp["w_in_main"], wp["w_in_pe"], wp["w_in_gate"], wp["g_q_a"], wp["w_uq_t"], wp["g_kv_a"], wp["w_uk"],
      wp["w_uv_t"], wp["g_q"], wp["g_k"], wp["g_k_sw"],
      jnp.kron(jnp.eye(2, dtype=BF16), jnp.ones((HEAD_PAD, HEAD_PAD), BF16)),
      tabs["e_cos"], tabs["e_sin"], tabs["cos_t"], tabs["sin_t"])


def _attn_kernel(qt_ref, k_ref, vt_ref, ot_ref, s0, s1, p0, p1, cm0, cm1, al0, al1):
    s_bufs = (s0, s1)
    p_bufs = (p0, p1)
    cmax_bufs = (cm0, cm1)
    alpha_bufs = (al0, al1)
    nq, _, tq = qt_ref.shape
    nk, _, tk = vt_ref.shape

    def score_chunk(qt, c, dst):
        sc = jnp.dot(k_ref[c * tk:(c + 1) * tk, :], qt, preferred_element_type=F32)
        dst[c] = sc
        return jnp.max(sc, axis=0, keepdims=True)

    def step(i, cur, with_scores=True):
        other = 1 - cur
        qt_next = qt_ref[i + 1] if with_scores else None
        m = jnp.full((1, tq), NEG_BIG, F32)
        acc = jnp.zeros((V_ROWS, tq), F32)
        for c in range(nk):
            m_new = jnp.maximum(m, cmax_bufs[cur][c:c + 1, :])
            alpha_bufs[cur][c:c + 1, :] = jnp.exp2(m - m_new)
            p_bufs[cur][c] = jnp.exp2(s_bufs[cur][c] - m_new).astype(BF16)
            m = m_new
            if with_scores:
                cmax_bufs[other][c:c + 1, :] = score_chunk(qt_next, c, s_bufs[other])
            acc = alpha_bufs[other][c:c + 1, :] * acc + jnp.dot(
                vt_ref[c], p_bufs[other][c], preferred_element_type=F32)
        ot_ref[jnp.maximum(i - 1, 0)] = (acc[0:V_DIM] / acc[V_DIM:V_DIM + 1]).astype(BF16)

    def trip(t, carry):
        step(2 * t, 0)
        step(2 * t + 1, 1)
        return carry

    qt0 = qt_ref[0]
    for c in range(nk):
        cm0[c:c + 1, :] = score_chunk(qt0, c, s0)
        p1[c] = jnp.ones((tk, tq), BF16)
    al1[...] = jnp.ones((nk, tq), F32)
    lax.fori_loop(0, nq // 2 - 1, trip, 0)
    step(nq - 2, 0)
    step(nq - 1, 1, with_scores=False)
    acc = jnp.zeros((V_ROWS, tq), F32)
    for c in range(nk):
        acc = al1[c:c + 1, :] * acc + jnp.dot(vt_ref[c], p1[c], preferred_element_type=F32)
    ot_ref[nq - 1] = (acc[0:V_DIM] / acc[V_DIM:V_DIM + 1]).astype(BF16)


def _attn_call(qt, k, vt):
    b, nq, _, tq = qt.shape
    _, nk, _, tk = vt.shape
    s = k.shape[1]
    assert nq % 2 == 0 and nq >= 4 and nk * tk == s
    return pl.pallas_call(
        _attn_kernel,
        out_shape=jax.ShapeDtypeStruct((b, nq, N_HEADS * V_DIM, tq), BF16),
        grid=(b, N_HEADS),
        in_specs=[
            pl.BlockSpec((None, nq, HEAD_PAD, tq), lambda bi, hh: (bi, 0, hh, 0)),
            pl.BlockSpec((None, s, HEAD_PAD), lambda bi, hh: (bi, 0, hh)),
            pl.BlockSpec((None, nk, V_ROWS, tk), lambda bi, hh: (bi, 0, hh, 0)),
        ],
        out_specs=pl.BlockSpec((None, nq, V_DIM, tq), lambda bi, hh: (bi, 0, hh, 0)),
        scratch_shapes=([pltpu.VMEM((nk, tk, tq), F32)] * 2 + [pltpu.VMEM((nk, tk, tq), BF16)] * 2
                        + [pltpu.VMEM((nk, tq), F32)] * 4),
        compiler_params=pltpu.CompilerParams(
            dimension_semantics=("parallel", "parallel"), vmem_limit_bytes=VMEM_LIMIT),
        name="attention_t",
    )(qt, k, vt)


def _merge_kernel(x_ref, cb_ref, u_ref, uprev_ref, unext_ref, sga_ref, sgb_ref, ot_ref,
                  cw_ref, wco_ref, wao_ref, wo_ref, gt_ref, xo_ref):
    i = pl.program_id(1)
    last = pl.num_programs(1) - 1
    u = u_ref[...].astype(F32)
    tm = u.shape[0]
    halo = uprev_ref.shape[0]
    prev = jnp.where(i > 0, uprev_ref[halo - 1:halo, :].astype(F32), 0.0)
    nxt = jnp.where(i < last, unext_ref[0:1, :].astype(F32), 0.0)
    row = lax.broadcasted_iota(jnp.int32, u.shape, 0)
    u_m = jnp.where(row == 0, prev, pltpu.roll(u, 1, 0))
    u_p = jnp.where(row == tm - 1, nxt, pltpu.roll(u, tm - 1, 0))
    cw = cw_ref[...]
    conv = cw[0:1, :] * u_m + cw[1:2, :] * u + cw[2:3, :] * u_p
    za = (cb_ref[...].astype(F32) * conv).astype(BF16)
    ya = jnp.dot(za, wco_ref[...], preferred_element_type=F32)
    tn = (((0,), (0,)), ((), ()))
    yb = jnp.concatenate(
        [lax.dot_general(ot_ref[j], wao_ref[...], tn, preferred_element_type=F32) for j in range(ot_ref.shape[0])],
        axis=0)
    merged = (sga_ref[...].astype(F32) * ya + sgb_ref[...].astype(F32) * yb).astype(BF16)
    y = jnp.dot(merged, wo_ref[...], preferred_element_type=F32)
    xo_ref[...] = x_ref[...] + gt_ref[...] * y


def _merge_call(layer, x, cb, u, sga, sgb, ot, gt1, wp):
    b, s, _ = x.shape
    lw = lambda *shape: _layer_spec(shape, layer)
    tm = TOKEN_TILE
    halo = 16
    nh = tm // halo
    tile = lambda w: pl.BlockSpec((None, tm, w), lambda bi, i: (bi, i, 0))
    per_batch = pl.BlockSpec((None, 1, D_MODEL), lambda bi, i: (bi, 0, 0))
    in_specs = [
        tile(D_MODEL), tile(CONV_WIDTH), tile(CONV_WIDTH),
        pl.BlockSpec((None, halo, CONV_WIDTH), lambda bi, i: (bi, jnp.maximum(i * nh - 1, 0), 0)),
        pl.BlockSpec((None, halo, CONV_WIDTH), lambda bi, i: (bi, jnp.minimum((i + 1) * nh, s // halo - 1), 0)),
        tile(D_MODEL), tile(D_MODEL),
        pl.BlockSpec((None, tm // ATTN_TILE, N_HEADS * V_DIM, ATTN_TILE), lambda bi, i: (bi, i, 0, 0)),
        lw(3, CONV_WIDTH), lw(CONV_WIDTH, D_MODEL), lw(N_HEADS * V_DIM, D_MODEL), lw(D_MODEL, D_MODEL), per_batch,
    ]
    return pl.pallas_call(
        _merge_kernel, out_shape=jax.ShapeDtypeStruct(x.shape, F32), grid=(b, s // tm),
        in_specs=in_specs, out_specs=tile(D_MODEL),
        compiler_params=pltpu.CompilerParams(
            dimension_semantics=("parallel", "parallel"), vmem_limit_bytes=VMEM_LIMIT),
        name="merge_out_proj",
    )(x, cb, u, u, u, sga, sgb, ot, wp["conv_w"], wp["w_conv_out"], wp["w_attn_out"], wp["w_o"], gt1)


def _route(logits_t, bias_col):
    scores = jax.nn.sigmoid(logits_t)
    biased = scores + bias_col
    neg_inf = jnp.float32(-jnp.inf)
    gscores = []
    for g in range(N_GROUPS):
        rws = [biased[g * GROUP_SIZE + j:g * GROUP_SIZE + j + 1] for j in range(GROUP_SIZE)]
        best = None
        for a in range(GROUP_SIZE):
            for bb in range(a + 1, GROUP_SIZE):
                pair = rws[a] + rws[bb]
                best = pair if best is None else jnp.maximum(best, pair)
        gscores.append(best)
    top = gscores[0]
    sel = jnp.zeros_like(top, dtype=jnp.int32)
    for g in range(1, N_GROUPS):
        better = gscores[g] > top
        sel = jnp.where(better, g, sel)
        top = jnp.where(better, gscores[g], top)
    eidx = lax.broadcasted_iota(jnp.int32, biased.shape, 0)
    masked = jnp.where(eidx // GROUP_SIZE == sel, biased, neg_inf)
    m1 = jnp.max(masked, axis=0, keepdims=True)
    i1 = jnp.min(jnp.where(masked == m1, eidx, N_EXPERTS), axis=0, keepdims=True)
    pick1 = eidx == i1
    masked2 = jnp.where(pick1, neg_inf, masked)
    m2 = jnp.max(masked2, axis=0, keepdims=True)
    i2 = jnp.min(jnp.where(masked2 == m2, eidx, N_EXPERTS), axis=0, keepdims=True)
    pick2 = eidx == i2
    s1 = jnp.sum(jnp.where(pick1, scores, 0.0), axis=0, keepdims=True)
    s2 = jnp.sum(jnp.where(pick2, scores, 0.0), axis=0, keepdims=True)
    return jnp.where(pick1 | pick2, scores / (s1 + s2), 0.0), sel


def _moe_kernel(x_ref, g_ref, sc_ref, sh_ref, gt_ref, wr_ref, br_ref, tri_ref,
                wg_ref, wu_ref, wdn_ref, wsg_ref, wsu_ref, wsdn_ref, xo_ref, hs_ref, cs_ref, ys_ref):
    x = x_ref[...]
    tm = x.shape[0]
    tpad = hs_ref.shape[0]
    r = lax.rsqrt(jnp.mean(x * x, axis=-1, keepdims=True) + EPS)
    h = (x * r) * (g_ref[...] * (1.0 + sc_ref[...])) + sh_ref[...]
    hi, lo = _split_bf16(h)

    nt = (((1,), (1,)), ((), ()))
    l_hi = lax.dot_general(wr_ref[...], hi, nt, preferred_element_type=F32)
    l_lo = lax.dot_general(wr_ref[0:N_EXPERTS, :], lo, nt, preferred_element_type=F32)
    logits_t = l_hi[0:N_EXPERTS] + l_hi[N_EXPERTS:2 * N_EXPERTS] + l_lo
    comb_t, sel = _route(logits_t, br_ref[...])
    comb = jnp.concatenate([comb_t, jnp.zeros((128 - N_EXPERTS, tm), F32)], axis=0).T

    member = (lax.broadcasted_iota(jnp.int32, (8, tm), 0) == sel).astype(F32)
    rank = jnp.dot(member.astype(BF16), tri_ref[...], preferred_element_type=F32)
    count = jnp.sum(member, axis=1, keepdims=True)
    nblk = jnp.floor((count + (ROW_BLOCK - 1)) * (1.0 / ROW_BLOCK))
    start = []
    rows_so_far = jnp.zeros((1, 1), F32)
    for g in range(N_GROUPS):
        start.append(rows_so_far)
        rows_so_far = rows_so_far + nblk[g:g + 1] * ROW_BLOCK
    dest = jnp.zeros((1, tm), F32)
    for g in range(N_GROUPS):
        dest = dest + member[g:g + 1] * (start[g] + rank[g:g + 1])
    perm = (lax.broadcasted_iota(jnp.int32, (tpad, tm), 0) == dest.astype(jnp.int32)).astype(BF16)

    hs_ref[...] = jnp.dot(perm, hi, preferred_element_type=F32).astype(BF16)
    c1 = comb.astype(BF16)
    rest = comb - c1.astype(F32)
    c2, c3 = _split_bf16(rest)
    packed = jnp.concatenate([c1[:, 0:N_EXPERTS], c2[:, 0:N_EXPERTS], c3[:, 0:N_EXPERTS],
                              jnp.zeros((tm, 128 - 3 * N_EXPERTS), BF16)], axis=1)
    cs_ref[...] = jnp.dot(perm, packed, preferred_element_type=F32)
    ys_ref[...] = jnp.zeros(ys_ref.shape, BF16)

    for g in range(N_GROUPS):
        row0 = jnp.sum(start[g]).astype(jnp.int32)
        nb = jnp.sum(nblk[g:g + 1]).astype(jnp.int32)

        def block(blk, carry, g=g, row0=row0):
            rows = pl.ds(pl.multiple_of(row0 + blk * ROW_BLOCK, ROW_BLOCK), ROW_BLOCK)
            hb = hs_ref[rows, :]
            cw = cs_ref[rows, :]
            y = jnp.zeros((ROW_BLOCK, D_MODEL), F32)
            for j in range(GROUP_SIZE):
                e = g * GROUP_SIZE + j
                gate = jnp.dot(hb, wg_ref[e], preferred_element_type=F32)
                up = jnp.dot(hb, wu_ref[e], preferred_element_type=F32)
                w = (cw[:, e:e + 1] + cw[:, N_EXPERTS + e:N_EXPERTS + e + 1]
                     + cw[:, 2 * N_EXPERTS + e:2 * N_EXPERTS + e + 1])
                y = y + jnp.dot((_silu(gate) * up * w).astype(BF16), wdn_ref[e * D_EXPERT:(e + 1) * D_EXPERT, :],
                                preferred_element_type=F32)
            ys_ref[rows, :] = y.astype(BF16)
            return carry

        lax.fori_loop(0, nb, block, 0)

    tn = (((0,), (0,)), ((), ()))
    routed = lax.dot_general(perm, ys_ref[...], tn, preferred_element_type=F32)
    a_s = _silu(jnp.dot(hi, wsg_ref[...], preferred_element_type=F32)) * jnp.dot(
        hi, wsu_ref[...], preferred_element_type=F32)
    y = routed + jnp.dot(a_s.astype(BF16), wsdn_ref[...], preferred_element_type=F32)
    xo_ref[...] = x + gt_ref[...] * y


def _moe_call(layer, x, sc2, sh2, gt2, wp, wr):
    b, s, _ = x.shape
    tm = TOKEN_TILE
    tile = pl.BlockSpec((None, tm, D_MODEL), lambda bi, i: (bi, i, 0))
    per_batch = pl.BlockSpec((None, 1, D_MODEL), lambda bi, i: (bi, 0, 0))
    lw = lambda *shape: _layer_spec(shape, layer)
    tpad = tm + N_GROUPS * ROW_BLOCK
    before = lax.broadcasted_iota(jnp.int32, (tm, tm), 0) < lax.broadcasted_iota(jnp.int32, (tm, tm), 1)
    in_specs = [
        tile, lw(1, D_MODEL), per_batch, per_batch, per_batch,
        _const_spec((2 * N_EXPERTS, D_MODEL)), _const_spec((N_EXPERTS, 1)), _const_spec((tm, tm)),
        lw(N_EXPERTS, D_MODEL, D_EXPERT), lw(N_EXPERTS, D_MODEL, D_EXPERT), lw(N_EXPERTS * D_EXPERT, D_MODEL),
        lw(D_MODEL, D_SHARED), lw(D_MODEL, D_SHARED), lw(D_SHARED, D_MODEL),
    ]
    return pl.pallas_call(
        _moe_kernel, out_shape=jax.ShapeDtypeStruct(x.shape, F32), grid=(b, s // tm),
        in_specs=in_specs, out_specs=tile,
        scratch_shapes=[pltpu.VMEM((tpad, D_MODEL), BF16), pltpu.VMEM((tpad, 128), F32),
                        pltpu.VMEM((tpad, D_MODEL), BF16)],
        compiler_params=pltpu.CompilerParams(
            dimension_semantics=("parallel", "parallel"), vmem_limit_bytes=VMEM_LIMIT),
        name="moe_ffn",
    )(x, wp["g_ffn"], sc2, sh2, gt2, wr["w_hl"], wr["b"], before.astype(BF16), wp["w_e_gate"], wp["w_e_up"],
      wp["w_e_down"], wp["w_s_gate"], wp["w_s_up"], wp["w_s_down"])


def _prep_weights(g_mix, g_ffn, w_in, conv_w, g_q_a, w_uq, g_kv_a, w_ukv, g_qn, g_kn, w_conv_out, w_attn_out,
                  w_o, w_e_gate, w_e_up, w_e_down, w_s_gate, w_s_up, w_s_down):
    depth = w_in.shape[0]
    split = 3 * CONV_WIDTH + Q_RANK + KV_RANK
    w_pe = w_in[:, :, split:split + QK_ROPE]
    w_pe_sw = jnp.concatenate([w_pe[:, :, ROPE_HALF:], w_pe[:, :, :ROPE_HALF]], axis=2)
    lane_pad = lambda w: jnp.pad(w, ((0, 0), (0, 0), (QK_NOPE, HEAD_PAD - QK_DIM)))
    w_in_pe = jnp.concatenate([lane_pad(w_pe), lane_pad(w_pe_sw)], axis=2).astype(BF16)

    wq = w_uq.reshape(depth, Q_RANK, N_HEADS, QK_DIM)
    wq = jnp.pad(wq, ((0, 0), (0, 0), (0, 0), (0, HEAD_PAD - QK_DIM))).reshape(depth, Q_RANK, N_HEADS * HEAD_PAD)
    wkv = w_ukv.reshape(depth, KV_RANK, N_HEADS, QK_NOPE + V_DIM)
    wk = jnp.pad(wkv[..., :QK_NOPE], ((0, 0), (0, 0), (0, 0), (0, HEAD_PAD - QK_NOPE)))
    wk = wk.reshape(depth, KV_RANK, N_HEADS * HEAD_PAD)
    wv = wkv[..., QK_NOPE:].reshape(depth, KV_RANK, N_HEADS * V_DIM)
    scale = QK_DIM ** -0.5 * LOG2_E
    g_q = jnp.pad(g_qn * scale, ((0, 0), (0, HEAD_PAD - QK_DIM))).reshape(depth, HEAD_PAD, 1)
    g_k = jnp.pad(g_kn, ((0, 0), (0, HEAD_PAD - QK_DIM))).reshape(depth, 1, HEAD_PAD)
    g_rope = g_kn[:, QK_NOPE:]
    g_k_sw = jnp.pad(jnp.concatenate([g_rope[:, ROPE_HALF:], g_rope[:, :ROPE_HALF]], axis=1),
                     ((0, 0), (QK_NOPE, HEAD_PAD - QK_DIM))).reshape(depth, 1, HEAD_PAD)
    return {
        "g_mix": g_mix.reshape(depth, 1, D_MODEL),
        "g_ffn": g_ffn.reshape(depth, 1, D_MODEL),
        "w_in_main": w_in[:, :, :split].astype(BF16),
        "w_in_pe": w_in_pe,
        "w_in_gate": w_in[:, :, split + QK_ROPE:].astype(BF16),
        "g_q_a": g_q_a.reshape(depth, 1, Q_RANK),
        "w_uq_t": wq.transpose(0, 2, 1).astype(BF16),
        "g_kv_a": g_kv_a.reshape(depth, 1, KV_RANK),
        "w_uk": wk.astype(BF16),
        "w_uv_t": wv.transpose(0, 2, 1).astype(BF16),
        "g_q": g_q, "g_k": g_k, "g_k_sw": g_k_sw,
        "conv_w": conv_w,
        "w_conv_out": w_conv_out.astype(BF16),
        "w_attn_out": w_attn_out.astype(BF16),
        "w_o": w_o.astype(BF16),
        "w_e_gate": w_e_gate.astype(BF16),
        "w_e_up": w_e_up.astype(BF16),
        "w_e_down": w_e_down.astype(BF16).reshape(depth, N_EXPERTS * D_EXPERT, D_MODEL),
        "w_s_gate": w_s_gate.astype(BF16),
        "w_s_up": w_s_up.astype(BF16),
        "w_s_down": w_s_down.astype(BF16),
    }


def _rope_tables(positions):
    inv_freq = ROPE_THETA ** (-jnp.arange(ROPE_HALF, dtype=F32) / ROPE_HALF)
    ang_t = inv_freq[None, :, None] * positions.astype(F32)[:, None, :]
    row = jnp.arange(ROPE_HALF)[:, None]
    lane = jnp.arange(HEAD_PAD)[None, :]
    first = lane == row + QK_NOPE
    second = lane == row + QK_NOPE + ROPE_HALF
    return {
        "cos_t": jnp.cos(ang_t), "sin_t": jnp.sin(ang_t),
        "e_cos": (first | second).astype(BF16),
        "e_sin": (second.astype(F32) - first.astype(F32)).astype(BF16),
    }


def kernel(x, c, positions, w_ada, b_ada, g_mix, g_ffn, w_in, conv_w, g_q_a, w_uq, g_kv_a, w_ukv,
           g_qn, g_kn, w_conv_out, w_attn_out, w_o, w_router, b_router, w_e_gate, w_e_up, w_e_down,
           w_s_gate, w_s_up, w_s_down):
    depth = w_ada.shape[0]
    b = x.shape[0]
    c_pad = jnp.pad(c, ((0, 8 - b), (0, 0)))
    mod = _modulation(c_pad, w_ada, b_ada)[:, :b, :]
    mod = mod.reshape(depth, b, 6, 1, D_MODEL)

    tabs = _rope_tables(positions)
    wr_hi, wr_lo = _split_bf16(w_router.T)
    wr = {"w_hl": jnp.concatenate([wr_hi, wr_lo], axis=0), "b": b_router.reshape(N_EXPERTS, 1)}
    wp = _prep_weights(g_mix, g_ffn, w_in, conv_w, g_q_a, w_uq, g_kv_a, w_ukv, g_qn, g_kn, w_conv_out,
                       w_attn_out, w_o, w_e_gate, w_e_up, w_e_down, w_s_gate, w_s_up, w_s_down)

    for l in range(depth):
        sh1, sc1, gt1, sh2, sc2, gt2 = (mod[l, :, j] for j in range(6))
        cb, u, sga, sgb, qt, k, vt = _proj_call(l, x, sc1, sh1, wp, tabs)
        ot = _attn_call(qt, k, vt)
        x = _merge_call(l, x, cb, u, sga, sgb, ot, gt1, wp)
        x = _moe_call(l, x, sc2, sh2, gt2, wp, wr)
    return x
```

```python
import functools

import jax
import jax.numpy as jnp
from jax import lax
from jax.experimental import pallas as pl
from jax.experimental.pallas import tpu as pltpu

F32 = jnp.float32
BF16 = jnp.bfloat16

D_MODEL = 1024
CONV_WIDTH = 512
N_HEADS = 8
QK_NOPE = 64
QK_ROPE = 32
V_DIM = 64
QK_DIM = QK_NOPE + QK_ROPE
Q_RANK = 256
KV_RANK = 128
ROPE_THETA = 10000.0
N_EXPERTS = 16
N_GROUPS = 4
GROUP_SIZE = N_EXPERTS // N_GROUPS
D_EXPERT = 256
D_SHARED = 256
EPS = 1e-6

HEAD_PAD = 128
V_ROWS = 80
LAT_START = 3 * CONV_WIDTH
ROPE_HALF = QK_ROPE // 2

TOKEN_TILE = 512
ATTN_TILE = 256
ROW_BLOCK = 160
LOG2_E = 1.4426950408889634
VMEM_LIMIT = 56 * 1024 * 1024
NEG_BIG = -1e30


def _const_spec(shape):
    nd = len(shape)
    return pl.BlockSpec(shape, lambda *_: (0,) * nd, pipeline_mode=pl.Buffered(1))


def _layer_spec(shape, layer):
    nd = len(shape)
    return pl.BlockSpec((None,) + tuple(shape), lambda *_: (layer,) + (0,) * nd, pipeline_mode=pl.Buffered(1))


def _silu(v):
    return v * jax.nn.sigmoid(v)


def _split_bf16(v):
    hi = v.astype(BF16)
    return hi, (v - hi.astype(F32)).astype(BF16)


def _mod_kernel(c_ref, w_ref, b_ref, o_ref):
    hi, lo = _split_bf16(_silu(c_ref[...]))
    w = w_ref[...].astype(BF16)
    o_ref[...] = (jnp.dot(hi, w, preferred_element_type=F32) + jnp.dot(lo, w, preferred_element_type=F32)
                  + b_ref[...])


def _modulation(c_pad, w_ada, b_ada):
    depth = w_ada.shape[0]
    width = 2 * D_MODEL
    nblk = w_ada.shape[2] // width
    rows = c_pad.shape[0]
    return pl.pallas_call(
        _mod_kernel,
        out_shape=jax.ShapeDtypeStruct((depth, rows, w_ada.shape[2]), F32),
        grid=(depth, nblk),
        in_specs=[
            pl.BlockSpec((rows, D_MODEL), lambda l, j: (0, 0)),
            pl.BlockSpec((None, D_MODEL, width), lambda l, j: (l, 0, j)),
            pl.BlockSpec((None, 1, width), lambda l, j: (l, 0, j)),
        ],
        out_specs=pl.BlockSpec((None, rows, width), lambda l, j: (l, 0, j)),
        compiler_params=pltpu.CompilerParams(
            dimension_semantics=("arbitrary", "arbitrary"), vmem_limit_bytes=VMEM_LIMIT),
        name="adaln_mod",
    )(c_pad, w_ada, b_ada.reshape(depth, 1, -1))


def _store_token_tiles(ref, row0, val):
    rows = val.shape[0]
    for j in range(val.shape[1] // ATTN_TILE):
        ref[j, row0:row0 + rows, :] = val[:, j * ATTN_TILE:(j + 1) * ATTN_TILE]


def _proj_kernel(x_ref, g_ref, sc_ref, sh_ref, wmain_ref, wpe_ref, wgate_ref, gqa_ref, wuqt_ref, gkva_ref, wuk_ref,
                 wuvt_ref, gq_ref, gk_ref, gksw_ref, ones_ref, ecos_ref, esin_ref, cost_ref, sint_ref,
                 cb_ref, u_ref, sga_ref, sgb_ref, qt_ref, k_ref, vt_ref):
    x = x_ref[...]
    r = lax.rsqrt(jnp.mean(x * x, axis=-1, keepdims=True) + EPS)
    h = (x * r) * (g_ref[...] * (1.0 + sc_ref[...])) + sh_ref[...]
    hb = h.astype(BF16)
    tm = x.shape[0]

    lat = jnp.dot(hb, wmain_ref[:, LAT_START:LAT_START + Q_RANK + KV_RANK], preferred_element_type=F32)
    cq = lat[:, 0:Q_RANK]
    ckv = lat[:, Q_RANK:Q_RANK + KV_RANK]
    pe = jnp.dot(hb, wpe_ref[...], preferred_element_type=F32)
    kpe = pe[:, 0:HEAD_PAD]
    kpe_sw = pe[:, HEAD_PAD:2 * HEAD_PAD]

    cqn = (cq * lax.rsqrt(jnp.mean(cq * cq, axis=-1, keepdims=True) + EPS) * gqa_ref[...]).astype(BF16)
    ckvn = (ckv * lax.rsqrt(jnp.mean(ckv * ckv, axis=-1, keepdims=True) + EPS) * gkva_ref[...]).astype(BF16)

    nt = (((1,), (1,)), ((), ()))
    qt = lax.dot_general(wuqt_ref[...], cqn, nt, preferred_element_type=F32)
    cost = cost_ref[...]
    sint = sint_ref[...]
    gq = gq_ref[...]
    for hh in range(N_HEADS):
        base = hh * HEAD_PAD
        qh = qt[base:base + HEAD_PAD]
        rr = lax.rsqrt(jnp.sum(qh * qh, axis=0, keepdims=True) * (1.0 / QK_DIM) + EPS)
        qn = qh * rr * gq
        x1 = qn[QK_NOPE:QK_NOPE + ROPE_HALF]
        x2 = qn[QK_NOPE + ROPE_HALF:QK_DIM]
        _store_token_tiles(qt_ref, base, qn[0:QK_NOPE].astype(BF16))
        _store_token_tiles(qt_ref, base + QK_NOPE, (x1 * cost - x2 * sint).astype(BF16))
        _store_token_tiles(qt_ref, base + QK_NOPE + ROPE_HALF, (x2 * cost + x1 * sint).astype(BF16))
        _store_token_tiles(qt_ref, base + QK_DIM, jnp.zeros((HEAD_PAD - QK_DIM, tm), BF16))

    kn = jnp.dot(ckvn, wuk_ref[...], preferred_element_type=F32)
    tn = (((0,), (0,)), ((), ()))
    place = lambda tab, e: sum(lax.dot_general(part, e, tn, preferred_element_type=F32) for part in _split_bf16(tab))
    lane = lax.broadcasted_iota(jnp.int32, (1, HEAD_PAD), 1)
    ck = jnp.where(lane < QK_NOPE, 1.0, 0.0) + place(cost, ecos_ref[...])
    ss = place(sint, esin_ref[...])
    direct = gk_ref[...] * ck
    partner = kpe_sw * (gksw_ref[...] * ss)
    ones = ones_ref[...]
    kpe2 = jnp.concatenate([kpe, kpe], axis=1)
    direct2 = jnp.concatenate([direct, direct], axis=1)
    partner2 = jnp.concatenate([partner, partner], axis=1)
    for pair in range(N_HEADS // 2):
        base = pair * 2 * HEAD_PAD
        kh = kn[:, base:base + 2 * HEAD_PAD] + kpe2
        ssq = jnp.dot((kh * kh).astype(BF16), ones, preferred_element_type=F32)
        rr = lax.rsqrt(ssq * (1.0 / QK_DIM) + EPS)
        k_ref[:, base:base + 2 * HEAD_PAD] = (rr * (kh * direct2 + partner2)).astype(BF16)

    vt = lax.dot_general(wuvt_ref[...], ckvn, nt, preferred_element_type=F32)
    for hh in range(N_HEADS):
        _store_token_tiles(vt_ref, hh * V_ROWS, vt[hh * V_DIM:(hh + 1) * V_DIM].astype(BF16))
        _store_token_tiles(vt_ref, hh * V_ROWS + V_DIM, jnp.ones((V_ROWS - V_DIM, tm), BF16))

    pc = jnp.dot(hb, wmain_ref[:, 0:LAT_START], preferred_element_type=F32)
    cb_ref[...] = pc[:, 0:CONV_WIDTH].astype(BF16)
    u_ref[...] = (pc[:, CONV_WIDTH:2 * CONV_WIDTH] * pc[:, 2 * CONV_WIDTH:3 * CONV_WIDTH]).astype(BF16)

    pg = jnp.dot(hb, wgate_ref[...], preferred_element_type=F32)
    sga_ref[...] = jax.nn.sigmoid(pg[:, 0:D_MODEL]).astype(BF16)
    sgb_ref[...] = jax.nn.sigmoid(pg[:, D_MODEL:2 * D_MODEL]).astype(BF16)


def _proj_call(layer, x, sc1, sh1, wp, tabs):
    b, s, _ = x.shape
    tm = TOKEN_TILE
    at = ATTN_TILE
    tile = lambda w: pl.BlockSpec((None, tm, w), lambda bi, i: (bi, i, 0))
    tile_t = lambda rws: pl.BlockSpec((None, rws, tm), lambda bi, i: (bi, 0, i))
    per_batch = pl.BlockSpec((None, 1, D_MODEL), lambda bi, i: (bi, 0, 0))
    lw = lambda *shape: _layer_spec(shape, layer)
    in_specs = [
        tile(D_MODEL), lw(1, D_MODEL), per_batch, per_batch,
        lw(D_MODEL, LAT_START + Q_RANK + KV_RANK), lw(D_MODEL, 2 * HEAD_PAD), lw(D_MODEL, 2 * D_MODEL), lw(1, Q_RANK),
        lw(N_HEADS * HEAD_PAD, Q_RANK), lw(1, KV_RANK),
        lw(KV_RANK, N_HEADS * HEAD_PAD), lw(N_HEADS * V_DIM, KV_RANK),
        lw(HEAD_PAD, 1), lw(1, HEAD_PAD), lw(1, HEAD_PAD),
        _const_spec((2 * HEAD_PAD, 2 * HEAD_PAD)), _const_spec((ROPE_HALF, HEAD_PAD)), _const_spec((ROPE_HALF, HEAD_PAD)),
        tile_t(ROPE_HALF), tile_t(ROPE_HALF),
    ]
    out_shape = [
        jax.ShapeDtypeStruct((b, s, CONV_WIDTH), BF16),
        jax.ShapeDtypeStruct((b, s, CONV_WIDTH), BF16),
        jax.ShapeDtypeStruct((b, s, D_MODEL), BF16),
        jax.ShapeDtypeStruct((b, s, D_MODEL), BF16),
        jax.ShapeDtypeStruct((b, s // at, N_HEADS * HEAD_PAD, at), BF16),
        jax.ShapeDtypeStruct((b, s, N_HEADS * HEAD_PAD), BF16),
        jax.ShapeDtypeStruct((b, s // at, N_HEADS * V_ROWS, at), BF16),
    ]
    att_tiles = lambda rws: pl.BlockSpec((None, tm // at, rws, at), lambda bi, i: (bi, i, 0, 0))
    out_specs = [tile(CONV_WIDTH), tile(CONV_WIDTH), tile(D_MODEL), tile(D_MODEL),
                 att_tiles(N_HEADS * HEAD_PAD), tile(N_HEADS * HEAD_PAD), att_tiles(N_HEADS * V_ROWS)]
    return pl.pallas_call(
        _proj_kernel, out_shape=out_shape, grid=(b, s // tm),
        in_specs=in_specs, out_specs=out_specs,
        compiler_params=pltpu.CompilerParams(
            dimension_semantics=("parallel", "parallel"), vmem_limit_bytes=VMEM_LIMIT),
        name="norm_in_proj",
    )(x, wp["g_mix"], sc1, sh1, wp["w_in_main"], wp["w_in_pe"], wp["w_in_gate"], wp["g_q_a"], wp["w_uq_t"], wp["g_kv_a"], wp["w_uk"],
      wp["w_uv_t"], wp["g_q"], wp["g_k"], wp["g_k_sw"],
      jnp.kron(jnp.eye(2, dtype=BF16), jnp.ones((HEAD_PAD, HEAD_PAD), BF16)),
      tabs["e_cos"], tabs["e_sin"], tabs["cos_t"], tabs["sin_t"])


def _attn_kernel(qt_ref, k_ref, vt_ref, ot_ref, s0, s1, p0, p1, cm0, cm1, al0, al1):
    s_bufs = (s0, s1)
    p_bufs = (p0, p1)
    cmax_bufs = (cm0, cm1)
    alpha_bufs = (al0, al1)
    nq, _, tq = qt_ref.shape
    nk, _, tk = vt_ref.shape

    def score_chunk(qt, c, dst):
        sc = jnp.dot(k_ref[c * tk:(c + 1) * tk, :], qt, preferred_element_type=F32)
        dst[c] = sc
        return jnp.max(sc, axis=0, keepdims=True)

    def step(i, cur, with_scores=True):
        other = 1 - cur
        qt_next = qt_ref[i + 1] if with_scores else None
        m = jnp.full((1, tq), NEG_BIG, F32)
        acc = jnp.zeros((V_ROWS, tq), F32)
        for c in range(nk):
            m_new = jnp.maximum(m, cmax_bufs[cur][c:c + 1, :])
            alpha_bufs[cur][c:c + 1, :] = jnp.exp2(m - m_new)
            p_bufs[cur][c] = jnp.exp2(s_bufs[cur][c] - m_new).astype(BF16)
            m = m_new
            if with_scores:
                cmax_bufs[other][c:c + 1, :] = score_chunk(qt_next, c, s_bufs[other])
            acc = alpha_bufs[other][c:c + 1, :] * acc + jnp.dot(
                vt_ref[c], p_bufs[other][c], preferred_element_type=F32)
        ot_ref[jnp.maximum(i - 1, 0)] = (acc[0:V_DIM] / acc[V_DIM:V_DIM + 1]).astype(BF16)

    def trip(t, carry):
        step(2 * t, 0)
        step(2 * t + 1, 1)
        return carry

    qt0 = qt_ref[0]
    for c in range(nk):
        cm0[c:c + 1, :] = score_chunk(qt0, c, s0)
        p1[c] = jnp.ones((tk, tq), BF16)
    al1[...] = jnp.ones((nk, tq), F32)
    lax.fori_loop(0, nq // 2 - 1, trip, 0)
    step(nq - 2, 0)
    step(nq - 1, 1, with_scores=False)
    acc = jnp.zeros((V_ROWS, tq), F32)
    for c in range(nk):
        acc = al1[c:c + 1, :] * acc + jnp.dot(vt_ref[c], p1[c], preferred_element_type=F32)
    ot_ref[nq - 1] = (acc[0:V_DIM] / acc[V_DIM:V_DIM + 1]).astype(BF16)


def _attn_call(qt, k, vt):
    b, nq, _, tq = qt.shape
    _, nk, _, tk = vt.shape
    s = k.shape[1]
    assert nq % 2 == 0 and nq >= 4 and nk * tk == s
    return pl.pallas_call(
        _attn_kernel,
        out_shape=jax.ShapeDtypeStruct((b, nq, N_HEADS * V_DIM, tq), BF16),
        grid=(b, N_HEADS),
        in_specs=[
            pl.BlockSpec((None, nq, HEAD_PAD, tq), lambda bi, hh: (bi, 0, hh, 0)),
            pl.BlockSpec((None, s, HEAD_PAD), lambda bi, hh: (bi, 0, hh)),
            pl.BlockSpec((None, nk, V_ROWS, tk), lambda bi, hh: (bi, 0, hh, 0)),
        ],
        out_specs=pl.BlockSpec((None, nq, V_DIM, tq), lambda bi, hh: (bi, 0, hh, 0)),
        scratch_shapes=([pltpu.VMEM((nk, tk, tq), F32)] * 2 + [pltpu.VMEM((nk, tk, tq), BF16)] * 2
                        + [pltpu.VMEM((nk, tq), F32)] * 4),
        compiler_params=pltpu.CompilerParams(
            dimension_semantics=("parallel", "parallel"), vmem_limit_bytes=VMEM_LIMIT),
        name="attention_t",
    )(qt, k, vt)


def _merge_kernel(x_ref, cb_ref, u_ref, uprev_ref, unext_ref, sga_ref, sgb_ref, ot_ref,
                  cw_ref, wco_ref, wao_ref, wo_ref, gt_ref, xo_ref):
    i = pl.program_id(1)
    last = pl.num_programs(1) - 1
    u = u_ref[...].astype(F32)
    tm = u.shape[0]
    halo = uprev_ref.shape[0]
    prev = jnp.where(i > 0, uprev_ref[halo - 1:halo, :].astype(F32), 0.0)
    nxt = jnp.where(i < last, unext_ref[0:1, :].astype(F32), 0.0)
    row = lax.broadcasted_iota(jnp.int32, u.shape, 0)
    u_m = jnp.where(row == 0, prev, pltpu.roll(u, 1, 0))
    u_p = jnp.where(row == tm - 1, nxt, pltpu.roll(u, tm - 1, 0))
    cw = cw_ref[...]
    conv = cw[0:1, :] * u_m + cw[1:2, :] * u + cw[2:3, :] * u_p
    za = (cb_ref[...].astype(F32) * conv).astype(BF16)
    ya = jnp.dot(za, wco_ref[...], preferred_element_type=F32)
    tn = (((0,), (0,)), ((), ()))
    yb = jnp.concatenate(
        [lax.dot_general(ot_ref[j], wao_ref[...], tn, preferred_element_type=F32) for j in range(ot_ref.shape[0])],
        axis=0)
    merged = (sga_ref[...].astype(F32) * ya + sgb_ref[...].astype(F32) * yb).astype(BF16)
    y = jnp.dot(merged, wo_ref[...], preferred_element_type=F32)
    xo_ref[...] = x_ref[...] + gt_ref[...] * y


def _merge_call(layer, x, cb, u, sga, sgb, ot, gt1, wp):
    b, s, _ = x.shape
    lw = lambda *shape: _layer_spec(shape, layer)
    tm = TOKEN_TILE
    halo = 16
    nh = tm // halo
    tile = lambda w: pl.BlockSpec((None, tm, w), lambda bi, i: (bi, i, 0))
    per_batch = pl.BlockSpec((None, 1, D_MODEL), lambda bi, i: (bi, 0, 0))
    in_specs = [
        tile(D_MODEL), tile(CONV_WIDTH), tile(CONV_WIDTH),
        pl.BlockSpec((None, halo, CONV_WIDTH), lambda bi, i: (bi, jnp.maximum(i * nh - 1, 0), 0)),
        pl.BlockSpec((None, halo, CONV_WIDTH), lambda bi, i: (bi, jnp.minimum((i + 1) * nh, s // halo - 1), 0)),
        tile(D_MODEL), tile(D_MODEL),
        pl.BlockSpec((None, tm // ATTN_TILE, N_HEADS * V_DIM, ATTN_TILE), lambda bi, i: (bi, i, 0, 0)),
        lw(3, CONV_WIDTH), lw(CONV_WIDTH, D_MODEL), lw(N_HEADS * V_DIM, D_MODEL), lw(D_MODEL, D_MODEL), per_batch,
    ]
    return pl.pallas_call(
        _merge_kernel, out_shape=jax.ShapeDtypeStruct(x.shape, F32), grid=(b, s // tm),
        in_specs=in_specs, out_specs=tile(D_MODEL),
        compiler_params=pltpu.CompilerParams(
            dimension_semantics=("parallel", "parallel"), vmem_limit_bytes=VMEM_LIMIT),
        name="merge_out_proj",
    )(x, cb, u, u, u, sga, sgb, ot, wp["conv_w"], wp["w_conv_out"], wp["w_attn_out"], wp["w_o"], gt1)


def _route(logits_t, bias_col):
    scores = jax.nn.sigmoid(logits_t)
    biased = scores + bias_col
    neg_inf = jnp.float32(-jnp.inf)
    gscores = []
    for g in range(N_GROUPS):
        rws = [biased[g * GROUP_SIZE + j:g * GROUP_SIZE + j + 1] for j in range(GROUP_SIZE)]
        best = None
        for a in range(GROUP_SIZE):
            for bb in range(a + 1, GROUP_SIZE):
                pair = rws[a] + rws[bb]
                best = pair if best is None else jnp.maximum(best, pair)
        gscores.append(best)
    top = gscores[0]
    sel = jnp.zeros_like(top, dtype=jnp.int32)
    for g in range(1, N_GROUPS):
        better = gscores[g] > top
        sel = jnp.where(better, g, sel)
        top = jnp.where(better, gscores[g], top)
    eidx = lax.broadcasted_iota(jnp.int32, biased.shape, 0)
    masked = jnp.where(eidx // GROUP_SIZE == sel, biased, neg_inf)
    m1 = jnp.max(masked, axis=0, keepdims=True)
    i1 = jnp.min(jnp.where(masked == m1, eidx, N_EXPERTS), axis=0, keepdims=True)
    pick1 = eidx == i1
    masked2 = jnp.where(pick1, neg_inf, masked)
    m2 = jnp.max(masked2, axis=0, keepdims=True)
    i2 = jnp.min(jnp.where(masked2 == m2, eidx, N_EXPERTS), axis=0, keepdims=True)
    pick2 = eidx == i2
    s1 = jnp.sum(jnp.where(pick1, scores, 0.0), axis=0, keepdims=True)
    s2 = jnp.sum(jnp.where(pick2, scores, 0.0), axis=0, keepdims=True)
    return jnp.where(pick1 | pick2, scores / (s1 + s2), 0.0), sel


def _moe_kernel(x_ref, g_ref, sc_ref, sh_ref, gt_ref, wr_ref, br_ref, tri_ref,
                wg_ref, wu_ref, wdn_ref, wsg_ref, wsu_ref, wsdn_ref, xo_ref, hs_ref, cs_ref, ys_ref):
    x = x_ref[...]
    tm = x.shape[0]
    tpad = hs_ref.shape[0]
    r = lax.rsqrt(jnp.mean(x * x, axis=-1, keepdims=True) + EPS)
    h = (x * r) * (g_ref[...] * (1.0 + sc_ref[...])) + sh_ref[...]
    hi, lo = _split_bf16(h)

    nt = (((1,), (1,)), ((), ()))
    l_hi = lax.dot_general(wr_ref[...], hi, nt, preferred_element_type=F32)
    l_lo = lax.dot_general(wr_ref[0:N_EXPERTS, :], lo, nt, preferred_element_type=F32)
    logits_t = l_hi[0:N_EXPERTS] + l_hi[N_EXPERTS:2 * N_EXPERTS] + l_lo
    comb_t, sel = _route(logits_t, br_ref[...])
    comb = jnp.concatenate([comb_t, jnp.zeros((128 - N_EXPERTS, tm), F32)], axis=0).T

    member = (lax.broadcasted_iota(jnp.int32, (8, tm), 0) == sel).astype(F32)
    rank = jnp.dot(member.astype(BF16), tri_ref[...], preferred_element_type=F32)
    count = jnp.sum(member, axis=1, keepdims=True)
    nblk = jnp.floor((count + (ROW_BLOCK - 1)) * (1.0 / ROW_BLOCK))
    start = []
    rows_so_far = jnp.zeros((1, 1), F32)
    for g in range(N_GROUPS):
        start.append(rows_so_far)
        rows_so_far = rows_so_far + nblk[g:g + 1] * ROW_BLOCK
    dest = jnp.zeros((1, tm), F32)
    for g in range(N_GROUPS):
        dest = dest + member[g:g + 1] * (start[g] + rank[g:g + 1])
    perm = (lax.broadcasted_iota(jnp.int32, (tpad, tm), 0) == dest.astype(jnp.int32)).astype(BF16)

    hs_ref[...] = jnp.dot(perm, hi, preferred_element_type=F32).astype(BF16)
    c1 = comb.astype(BF16)
    rest = comb - c1.astype(F32)
    c2, c3 = _split_bf16(rest)
    packed = jnp.concatenate([c1[:, 0:N_EXPERTS], c2[:, 0:N_EXPERTS], c3[:, 0:N_EXPERTS],
                              jnp.zeros((tm, 128 - 3 * N_EXPERTS), BF16)], axis=1)
    cs_ref[...] = jnp.dot(perm, packed, preferred_element_type=F32)
    ys_ref[...] = jnp.zeros(ys_ref.shape, BF16)

    for g in range(N_GROUPS):
        row0 = jnp.sum(start[g]).astype(jnp.int32)
        nb = jnp.sum(nblk[g:g + 1]).astype(jnp.int32)

        def block(blk, carry, g=g, row0=row0):
            rows = pl.ds(pl.multiple_of(row0 + blk * ROW_BLOCK, ROW_BLOCK), ROW_BLOCK)
            hb = hs_ref[rows, :]
            cw = cs_ref[rows, :]
            y = jnp.zeros((ROW_BLOCK, D_MODEL), F32)
            for j in range(GROUP_SIZE):
                e = g * GROUP_SIZE + j
                gate = jnp.dot(hb, wg_ref[e], preferred_element_type=F32)
                up = jnp.dot(hb, wu_ref[e], preferred_element_type=F32)
                w = (cw[:, e:e + 1] + cw[:, N_EXPERTS + e:N_EXPERTS + e + 1]
                     + cw[:, 2 * N_EXPERTS + e:2 * N_EXPERTS + e + 1])
                y = y + jnp.dot((_silu(gate) * up * w).astype(BF16), wdn_ref[e * D_EXPERT:(e + 1) * D_EXPERT, :],
                                preferred_element_type=F32)
            ys_ref[rows, :] = y.astype(BF16)
            return carry

        lax.fori_loop(0, nb, block, 0)

    tn = (((0,), (0,)), ((), ()))
    routed = lax.dot_general(perm, ys_ref[...], tn, preferred_element_type=F32)
    a_s = _silu(jnp.dot(hi, wsg_ref[...], preferred_element_type=F32)) * jnp.dot(
        hi, wsu_ref[...], preferred_element_type=F32)
    y = routed + jnp.dot(a_s.astype(BF16), wsdn_ref[...], preferred_element_type=F32)
    xo_ref[...] = x + gt_ref[...] * y


def _moe_call(layer, x, sc2, sh2, gt2, wp, wr):
    b, s, _ = x.shape
    tm = TOKEN_TILE
    tile = pl.BlockSpec((None, tm, D_MODEL), lambda bi, i: (bi, i, 0))
    per_batch = pl.BlockSpec((None, 1, D_MODEL), lambda bi, i: (bi, 0, 0))
    lw = lambda *shape: _layer_spec(shape, layer)
    tpad = tm + N_GROUPS * ROW_BLOCK
    before = lax.broadcasted_iota(jnp.int32, (tm, tm), 0) < lax.broadcasted_iota(jnp.int32, (tm, tm), 1)
    in_specs = [
        tile, lw(1, D_MODEL), per_batch, per_batch, per_batch,
        _const_spec((2 * N_EXPERTS, D_MODEL)), _const_spec((N_EXPERTS, 1)), _const_spec((tm, tm)),
        lw(N_EXPERTS, D_MODEL, D_EXPERT), lw(N_EXPERTS, D_MODEL, D_EXPERT), lw(N_EXPERTS * D_EXPERT, D_MODEL),
        lw(D_MODEL, D_SHARED), lw(D_MODEL, D_SHARED), lw(D_SHARED, D_MODEL),
    ]
    return pl.pallas_call(
        _moe_kernel, out_shape=jax.ShapeDtypeStruct(x.shape, F32), grid=(b, s // tm),
        in_specs=in_specs, out_specs=tile,
        scratch_shapes=[pltpu.VMEM((tpad, D_MODEL), BF16), pltpu.VMEM((tpad, 128), F32),
                        pltpu.VMEM((tpad, D_MODEL), BF16)],
        compiler_params=pltpu.CompilerParams(
            dimension_semantics=("parallel", "parallel"), vmem_limit_bytes=VMEM_LIMIT),
        name="moe_ffn",
    )(x, wp["g_ffn"], sc2, sh2, gt2, wr["w_hl"], wr["b"], before.astype(BF16), wp["w_e_gate"], wp["w_e_up"],
      wp["w_e_down"], wp["w_s_gate"], wp["w_s_up"], wp["w_s_down"])


def _prep_weights(g_mix, g_ffn, w_in, conv_w, g_q_a, w_uq, g_kv_a, w_ukv, g_qn, g_kn, w_conv_out, w_attn_out,
                  w_o, w_e_gate, w_e_up, w_e_down, w_s_gate, w_s_up, w_s_down):
    depth = w_in.shape[0]
    split = 3 * CONV_WIDTH + Q_RANK + KV_RANK
    w_pe = w_in[:, :, split:split + QK_ROPE]
    w_pe_sw = jnp.concatenate([w_pe[:, :, ROPE_HALF:], w_pe[:, :, :ROPE_HALF]], axis=2)
    lane_pad = lambda w: jnp.pad(w, ((0, 0), (0, 0), (QK_NOPE, HEAD_PAD - QK_DIM)))
    w_in_pe = jnp.concatenate([lane_pad(w_pe), lane_pad(w_pe_sw)], axis=2).astype(BF16)

    wq = w_uq.reshape(depth, Q_RANK, N_HEADS, QK_DIM)
    wq = jnp.pad(wq, ((0, 0), (0, 0), (0, 0), (0, HEAD_PAD - QK_DIM))).reshape(depth, Q_RANK, N_HEADS * HEAD_PAD)
    wkv = w_ukv.reshape(depth, KV_RANK, N_HEADS, QK_NOPE + V_DIM)
    wk = jnp.pad(wkv[..., :QK_NOPE], ((0, 0), (0, 0), (0, 0), (0, HEAD_PAD - QK_NOPE)))
    wk = wk.reshape(depth, KV_RANK, N_HEADS * HEAD_PAD)
    wv = wkv[..., QK_NOPE:].reshape(depth, KV_RANK, N_HEADS * V_DIM)
    scale = QK_DIM ** -0.5 * LOG2_E
    g_q = jnp.pad(g_qn * scale, ((0, 0), (0, HEAD_PAD - QK_DIM))).reshape(depth, HEAD_PAD, 1)
    g_k = jnp.pad(g_kn, ((0, 0), (0, HEAD_PAD - QK_DIM))).reshape(depth, 1, HEAD_PAD)
    g_rope = g_kn[:, QK_NOPE:]
    g_k_sw = jnp.pad(jnp.concatenate([g_rope[:, ROPE_HALF:], g_rope[:, :ROPE_HALF]], axis=1),
                     ((0, 0), (QK_NOPE, HEAD_PAD - QK_DIM))).reshape(depth, 1, HEAD_PAD)
    return {
        "g_mix": g_mix.reshape(depth, 1, D_MODEL),
        "g_ffn": g_ffn.reshape(depth, 1, D_MODEL),
        "w_in_main": w_in[:, :, :split].astype(BF16),
        "w_in_pe": w_in_pe,
        "w_in_gate": w_in[:, :, split + QK_ROPE:].astype(BF16),
        "g_q_a": g_q_a.reshape(depth, 1, Q_RANK),
        "w_uq_t": wq.transpose(0, 2, 1).astype(BF16),
        "g_kv_a": g_kv_a.reshape(depth, 1, KV_RANK),
        "w_uk": wk.astype(BF16),
        "w_uv_t": wv.transpose(0, 2, 1).astype(BF16),
        "g_q": g_q, "g_k": g_k, "g_k_sw": g_k_sw,
        "conv_w": conv_w,
        "w_conv_out": w_conv_out.astype(BF16),
        "w_attn_out": w_attn_out.astype(BF16),
        "w_o": w_o.astype(BF16),
        "w_e_gate": w_e_gate.astype(BF16),
        "w_e_up": w_e_up.astype(BF16),
        "w_e_down": w_e_down.astype(BF16).reshape(depth, N_EXPERTS * D_EXPERT, D_MODEL),
        "w_s_gate": w_s_gate.astype(BF16),
        "w_s_up": w_s_up.astype(BF16),
        "w_s_down": w_s_down.astype(BF16),
    }


def _rope_tables(positions):
    inv_freq = ROPE_THETA ** (-jnp.arange(ROPE_HALF, dtype=F32) / ROPE_HALF)
    ang_t = inv_freq[None, :, None] * positions.astype(F32)[:, None, :]
    row = jnp.arange(ROPE_HALF)[:, None]
    lane = jnp.arange(HEAD_PAD)[None, :]
    first = lane == row + QK_NOPE
    second = lane == row + QK_NOPE + ROPE_HALF
    return {
        "cos_t": jnp.cos(ang_t), "sin_t": jnp.sin(ang_t),
        "e_cos": (first | second).astype(BF16),
        "e_sin": (second.astype(F32) - first.astype(F32)).astype(BF16),
    }


def kernel(x, c, positions, w_ada, b_ada, g_mix, g_ffn, w_in, conv_w, g_q_a, w_uq, g_kv_a, w_ukv,
           g_qn, g_kn, w_conv_out, w_attn_out, w_o, w_router, b_router, w_e_gate, w_e_up, w_e_down,
           w_s_gate, w_s_up, w_s_down):
    depth = w_ada.shape[0]
    b = x.shape[0]
    c_pad = jnp.pad(c, ((0, 8 - b), (0, 0)))
    mod = _modulation(c_pad, w_ada, b_ada)[:, :b, :]
    mod = mod.reshape(depth, b, 6, 1, D_MODEL)

    tabs = _rope_tables(positions)
    wr_hi, wr_lo = _split_bf16(w_router.T)
    wr = {"w_hl": jnp.concatenate([wr_hi, wr_lo], axis=0), "b": b_router.reshape(N_EXPERTS, 1)}
    wp = _prep_weights(g_mix, g_ffn, w_in, conv_w, g_q_a, w_uq, g_kv_a, w_ukv, g_qn, g_kn, w_conv_out,
                       w_attn_out, w_o, w_e_gate, w_e_up, w_e_down, w_s_gate, w_s_up, w_s_down)

    for l in range(depth):
        sh1, sc1, gt1, sh2, sc2, gt2 = (mod[l, :, j] for j in range(6))
        cb, u, sga, sgb, qt, k, vt = _proj_call(l, x, sc1, sh1, wp, tabs)
        ot = _attn_call(qt, k, vt)
        x = _merge_call(l, x, cb, u, sga, sgb, ot, gt1, wp)
        x = _moe_call(l, x, sc2, sh2, gt2, wp, wr)
    return x
```
